```python
import math
import jax, jax.numpy as jnp
from jax import lax
import numpy as np

D_MODEL = 1024
BATCH = 8
SEQ = 2048
DEPTH = 2
DEC_BATCH = 32
DEC_SEQ = 8
PAST_LEN = 16384
PAGE_SIZE = 128

N_EVEN = (DEPTH + 1) // 2
N_ODD = DEPTH // 2
N_MEM = 256
H_A = 4
DK_A = 64
DV_A = 128
H_B = 4
DK_B = 64
DV_B = 128
GLA_RANK = 16
GLA_TAU = 16.0
CHUNK = 64
H_C = 8
DH_C = 64
Q_BLOCK = 128
H_X = 4
DH_X = D_MODEL // H_X
D_FF = 2816
MIX_W = H_A * DV_A + H_B * DV_B
EPS = 1e-6
F32 = jnp.float32

EVEN_SPLITS = [H_A * DK_A, H_A * DK_A, H_A * DV_A, H_A * DV_A,
               H_B * DK_B, H_B * DK_B, H_B * DV_B, H_B * DV_B, GLA_RANK]
EVEN_IN = int(sum(EVEN_SPLITS))
EVEN_CUTS = tuple(int(c) for c in np.cumsum(EVEN_SPLITS)[:-1])
ODD_IN = 3 * H_C * 2 * DH_C

kernel_name = "hybrid_ret_gla_diffattn_decode_step"


def rmsnorm(x, g):
    xf = x.astype(F32)
    y = xf * lax.rsqrt(jnp.mean(xf * xf, axis=-1, keepdims=True) + EPS)
    return (y * g.astype(F32)).astype(x.dtype)


def head_rms(o):
    return o * lax.rsqrt(jnp.mean(o * o, axis=-1, keepdims=True) + EPS)


def swiglu(x, wg, wu, wd):
    return (jax.nn.silu(x @ wg) * (x @ wu)) @ wd


def gated_linear_chunked(q, k, v, log_a, s0):
    B, L, H, DK = q.shape
    DV = v.shape[-1]
    C = math.gcd(L, CHUNK)
    N = L // C
    qc = q.astype(F32).reshape(B, N, C, H, DK)
    kc = k.astype(F32).reshape(B, N, C, H, DK)
    vc = v.astype(F32).reshape(B, N, C, H, DV)
    b = jnp.cumsum(log_a.astype(F32).reshape(B, N, C, H, DK), axis=2)
    b_last = b[:, :, -1:]
    q_dec = qc * jnp.exp(b)
    k_in = kc * jnp.exp(-b)
    k_out = kc * jnp.exp(b_last - b)
    causal = jnp.tril(jnp.ones((C, C), dtype=bool))
    att = jnp.where(causal, jnp.einsum('bnihd,bnjhd->bnhij', q_dec, k_in), 0.0)
    o_intra = jnp.einsum('bnhij,bnjhv->bnihv', att, vc)
    delta = jnp.einsum('bnjhd,bnjhv->bnhdv', k_out, vc)
    decay = jnp.exp(b_last[:, :, 0])

    def step(s, inp):
        dec, dl = inp
        return dec[..., None] * s + dl, s

    s_fin, s_prev = lax.scan(step, s0.astype(F32),
                             (jnp.moveaxis(decay, 1, 0), jnp.moveaxis(delta, 1, 0)))
    s_prev = jnp.moveaxis(s_prev, 0, 1)
    o_inter = jnp.einsum('bnihd,bnhdv->bnihv', q_dec, s_prev)
    return (o_intra + o_inter).reshape(B, L, H, DV), s_fin


def retention_gla_mixer(h, s_ret, s_gla, w_in, w_a2, b_a, g_ret, g_gla, w_out):
    B, L, _ = h.shape
    z = h @ w_in
    rq, rk, rv, rg, gq, gk, gv, gg, glr = jnp.split(z, EVEN_CUTS, axis=-1)
    log_gamma = jnp.log1p(-jnp.exp2(-5.0 - jnp.arange(H_A, dtype=F32)))
    log_ar = jnp.broadcast_to(log_gamma[:, None], (B, L, H_A, DK_A))
    o_r, s_ret_new = gated_linear_chunked(rq.reshape(B, L, H_A, DK_A),
                                          rk.reshape(B, L, H_A, DK_A) * DK_A ** -0.5,
                                          rv.reshape(B, L, H_A, DV_A), log_ar, s_ret)
    log_ag = jax.nn.log_sigmoid((glr @ w_a2 + b_a).astype(F32)) / GLA_TAU
    o_g, s_gla_new = gated_linear_chunked(gq.reshape(B, L, H_B, DK_B) * DK_B ** -0.5,
                                          gk.reshape(B, L, H_B, DK_B),
                                          gv.reshape(B, L, H_B, DV_B),
                                          log_ag.reshape(B, L, H_B, DK_B), s_gla)
    o_r = head_rms(o_r).reshape(B, L, H_A * DV_A) * g_ret.astype(F32) * jax.nn.silu(rg.astype(F32))
    o_g = head_rms(o_g).reshape(B, L, H_B * DV_B) * g_gla.astype(F32) * jax.nn.silu(gg.astype(F32))
    y = jnp.concatenate([o_r, o_g], axis=-1).astype(h.dtype) @ w_out
    return y, s_ret_new, s_gla_new


def alibi_slopes():
    return jnp.exp2(-8.0 * jnp.arange(1, H_C + 1, dtype=F32) / H_C)


def alibi_bias(qpos, kpos):
    dist = (qpos[:, None] - kpos[None, :]).astype(F32)
    return jnp.where(dist >= 0, -alibi_slopes()[:, None, None] * dist, -jnp.inf)


def diff_qkv(h, w_in):
    B, L, _ = h.shape
    q, k, v = jnp.split(h @ w_in, 3, axis=-1)
    return (q.reshape(B, L, H_C, 2, DH_C), k.reshape(B, L, H_C, 2, DH_C),
            v.reshape(B, L, H_C, 2 * DH_C))


def diff_lambda(lq1, lk1, lq2, lk2, lam_init):
    return (jnp.exp(jnp.sum(lq1.astype(F32) * lk1.astype(F32)))
            - jnp.exp(jnp.sum(lq2.astype(F32) * lk2.astype(F32))) + lam_init)


def diff_attn_prompt(q, k, v, lam):
    B, L = q.shape[:2]
    qb = math.gcd(L, Q_BLOCK)
    nq = L // qb
    kf = k.astype(F32)
    vf = v.astype(F32)
    kpos = jnp.arange(L)
    q_blocks = jnp.moveaxis((q.astype(F32) * DH_C ** -0.5).reshape(B, nq, qb, H_C, 2, DH_C), 1, 0)

    def block(args):
        q_blk, i = args
        qpos = i * qb + jnp.arange(qb)
        s = jnp.einsum('bqhmd,bkhmd->bmhqk', q_blk, kf) + alibi_bias(qpos, kpos)
        p = jax.nn.softmax(s, axis=-1)
        a = p[:, 0] - lam * p[:, 1]
        return jnp.einsum('bhqk,bkhe->bqhe', a, vf)

    o = lax.map(block, (q_blocks, jnp.arange(nq)))
    return jnp.moveaxis(o, 0, 1).reshape(B, L, H_C, 2 * DH_C)


def diff_attn_sample(q, k_new, v_new, cache_k, cache_v, layer, page_table, lam):
    B, T = q.shape[:2]
    n_pages = page_table.shape[1]
    qpos = n_pages * PAGE_SIZE + jnp.arange(T)
    qf = q.astype(F32) * DH_C ** -0.5

    def scores(k_blk, kpos):
        return jnp.einsum('bqhmd,bkhmd->bmhqk', qf, k_blk.astype(F32)) + alibi_bias(qpos, kpos)

    def merge(carry, s, v_blk):
        m, l, acc = carry
        m_new = jnp.maximum(m, jnp.max(s, axis=-1))
        corr = jnp.exp(m - m_new)
        p = jnp.exp(s - m_new[..., None])
        l = l * corr + jnp.sum(p, axis=-1)
        acc = acc * corr[..., None] + jnp.einsum('bmhqk,bkhe->bmhqe', p, v_blk.astype(F32))
        return (m_new, l, acc)

    def page_step(carry, j):
        phys = page_table[:, j]
        k_blk = cache_k[layer, phys].reshape(B, PAGE_SIZE, H_C, 2, DH_C)
        v_blk = cache_v[layer, phys]
        kpos = j * PAGE_SIZE + jnp.arange(PAGE_SIZE)
        return merge(carry, scores(k_blk, kpos), v_blk), None

    init = (jnp.full((B, 2, H_C, T), -jnp.inf, F32), jnp.zeros((B, 2, H_C, T), F32),
            jnp.zeros((B, 2, H_C, T, 2 * DH_C), F32))
    carry, _ = lax.scan(page_step, init, jnp.arange(n_pages))
    _, l, acc = merge(carry, scores(k_new, qpos), v_new)
    o = acc / l[..., None]
    return jnp.transpose(o[:, 0] - lam * o[:, 1], (0, 2, 1, 3))


def diff_out(o, gain, lam_init, w_out, dtype):
    B, L = o.shape[:2]
    o = head_rms(o) * gain.astype(F32) * (1.0 - lam_init)
    return o.reshape(B, L, H_C * 2 * DH_C).astype(dtype) @ w_out


def cross_attn(h, mk, mv, wq, wo):
    B, L, _ = h.shape
    q = (h @ wq).reshape(B, L, H_X, DH_X).astype(F32)
    s = jnp.einsum('blhd,bmhd->bhlm', q, mk.astype(F32)) * DH_X ** -0.5
    p = jax.nn.softmax(s, axis=-1)
    o = jnp.einsum('bhlm,bmhd->blhd', p, mv.astype(F32)).reshape(B, L, D_MODEL)
    return o.astype(h.dtype) @ wo


def setup_inputs(seed: int = 0) -> dict:
    key = jax.random.key(seed)
    keys = iter(jax.random.split(key, 64))

    def nrm(shape, scale):
        return jax.random.normal(next(keys), shape, F32) * scale

    def gain(shape):
        return 1.0 + nrm(shape, 0.05)

    n_pages = PAST_LEN // PAGE_SIZE
    n_used = DEC_BATCH * n_pages
    n_pool = n_used + max(1, n_used // 4)
    page_table = jax.random.permutation(next(keys), n_pool)[:n_used].reshape(DEC_BATCH, n_pages).astype(jnp.int32)
    d_in = D_MODEL ** -0.5
    return {
        "x_prompt": nrm((BATCH, SEQ, D_MODEL), 1.0),
        "x_sample": nrm((DEC_BATCH, DEC_SEQ, D_MODEL), 1.0),
        "state_ret": nrm((N_EVEN, DEC_BATCH, H_A, DK_A, DV_A), 0.5),
        "state_gla": nrm((N_EVEN, DEC_BATCH, H_B, DK_B, DV_B), 0.5),
        "cache_k": nrm((N_ODD, n_pool, PAGE_SIZE, H_C, 2 * DH_C), 1.0),
        "cache_v": nrm((N_ODD, n_pool, PAGE_SIZE, H_C, 2 * DH_C), 1.0),
        "cache_mem_k": nrm((DEPTH, DEC_BATCH, N_MEM, H_X, DH_X), 1.0),
        "cache_mem_v": nrm((DEPTH, DEC_BATCH, N_MEM, H_X, DH_X), 1.0),
        "page_table": page_table,
        "mem_prompt": nrm((BATCH, N_MEM, D_MODEL), 1.0),
        "norm_ffn1": gain((DEPTH, D_MODEL)),
        "ffn1_wg": nrm((DEPTH, D_MODEL, D_FF), d_in),
        "ffn1_wu": nrm((DEPTH, D_MODEL, D_FF), d_in),
        "ffn1_wd": nrm((DEPTH, D_FF, D_MODEL), D_FF ** -0.5),
        "norm_mix": gain((DEPTH, D_MODEL)),
        "w_in_even": nrm((N_EVEN, D_MODEL, EVEN_IN), d_in),
        "gla_wa2": nrm((N_EVEN, GLA_RANK, H_B * DK_B), GLA_RANK ** -0.5),
        "gla_ba": nrm((N_EVEN, H_B * DK_B), 0.1),
        "ret_gain": gain((N_EVEN, H_A * DV_A)),
        "gla_gain": gain((N_EVEN, H_B * DV_B)),
        "w_in_odd": nrm((N_ODD, D_MODEL, ODD_IN), d_in),
        "lam_q1": nrm((N_ODD, DH_C), 0.1),
        "lam_k1": nrm((N_ODD, DH_C), 0.1),
        "lam_q2": nrm((N_ODD, DH_C), 0.1),
        "lam_k2": nrm((N_ODD, DH_C), 0.1),
        "diff_gain": gain((N_ODD, 2 * DH_C)),
        "w_mix_out": nrm((DEPTH, MIX_W, D_MODEL), MIX_W ** -0.5),
        "norm_x": gain((DEPTH, D_MODEL)),
        "x_wq": nrm((DEPTH, D_MODEL, D_MODEL), d_in),
        "x_wk": nrm((DEPTH, D_MODEL, D_MODEL), d_in),
        "x_wv": nrm((DEPTH, D_MODEL, D_MODEL), d_in),
        "x_wo": nrm((DEPTH, D_MODEL, D_MODEL), d_in),
        "norm_ffn2": gain((DEPTH, D_MODEL)),
        "ffn2_wg": nrm((DEPTH, D_MODEL, D_FF), d_in),
        "ffn2_wu": nrm((DEPTH, D_MODEL, D_FF), d_in),
        "ffn2_wd": nrm((DEPTH, D_FF, D_MODEL), D_FF ** -0.5),
        "final_norm": gain((D_MODEL,)),
    }


def reference(x_prompt, x_sample, state_ret, state_gla, cache_k, cache_v, cache_mem_k, cache_mem_v,
              page_table, mem_prompt, norm_ffn1, ffn1_wg, ffn1_wu, ffn1_wd, norm_mix, w_in_even,
              gla_wa2, gla_ba, ret_gain, gla_gain, w_in_odd, lam_q1, lam_k1, lam_q2, lam_k2, diff_gain,
              w_mix_out, norm_x, x_wq, x_wk, x_wv, x_wo, norm_ffn2, ffn2_wg, ffn2_wu, ffn2_wd, final_norm):
    xp, xs = x_prompt, x_sample
    bp, lp, _ = xp.shape
    bs, ls, _ = xs.shape
    ret_p, ret_s, gla_p, gla_s = [], [], [], []
    kr_p, vr_p, kr_s, vr_s = [], [], [], []
    mk_p, mv_p = [], []
    for li in range(DEPTH):
        xp = xp + 0.5 * swiglu(rmsnorm(xp, norm_ffn1[li]), ffn1_wg[li], ffn1_wu[li], ffn1_wd[li])
        xs = xs + 0.5 * swiglu(rmsnorm(xs, norm_ffn1[li]), ffn1_wg[li], ffn1_wu[li], ffn1_wd[li])
        hp = rmsnorm(xp, norm_mix[li])
        hs = rmsnorm(xs, norm_mix[li])
        if li % 2 == 0:
            e = li // 2
            prm = (w_in_even[e], gla_wa2[e], gla_ba[e], ret_gain[e], gla_gain[e], w_mix_out[li])
            zr = jnp.zeros((bp, H_A, DK_A, DV_A), F32)
            zg = jnp.zeros((bp, H_B, DK_B, DV_B), F32)
            mp, sr_p, sg_p = retention_gla_mixer(hp, zr, zg, *prm)
            ms, sr_s, sg_s = retention_gla_mixer(hs, state_ret[e], state_gla[e], *prm)
            ret_p.append(sr_p); ret_s.append(sr_s); gla_p.append(sg_p); gla_s.append(sg_s)
        else:
            o = li // 2
            lam_init = 0.8 - 0.6 * math.exp(-0.3 * li)
            lam = diff_lambda(lam_q1[o], lam_k1[o], lam_q2[o], lam_k2[o], lam_init)
            qp, kp, vp = diff_qkv(hp, w_in_odd[o])
            qs, ks, vs = diff_qkv(hs, w_in_odd[o])
            op = diff_attn_prompt(qp, kp, vp, lam)
            os_ = diff_attn_sample(qs, ks, vs, cache_k, cache_v, o, page_table, lam)
            mp = diff_out(op, diff_gain[o], lam_init, w_mix_out[li], xp.dtype)
            ms = diff_out(os_, diff_gain[o], lam_init, w_mix_out[li], xs.dtype)
            kr_p.append(kp.reshape(bp, lp, H_C, 2 * DH_C)); vr_p.append(vp)
            kr_s.append(ks.reshape(bs, ls, H_C, 2 * DH_C)); vr_s.append(vs)
        xp = xp + mp
        xs = xs + ms
        mk = (mem_prompt @ x_wk[li]).reshape(bp, N_MEM, H_X, DH_X)
        mv = (mem_prompt @ x_wv[li]).reshape(bp, N_MEM, H_X, DH_X)
        xp = xp + cross_attn(rmsnorm(xp, norm_x[li]), mk, mv, x_wq[li], x_wo[li])
        xs = xs + cross_attn(rmsnorm(xs, norm_x[li]), cache_mem_k[li], cache_mem_v[li], x_wq[li], x_wo[li])
        mk_p.append(mk); mv_p.append(mv)
        xp = xp + 0.5 * swiglu(rmsnorm(xp, norm_ffn2[li]), ffn2_wg[li], ffn2_wu[li], ffn2_wd[li])
        xs = xs + 0.5 * swiglu(rmsnorm(xs, norm_ffn2[li]), ffn2_wg[li], ffn2_wu[li], ffn2_wd[li])
    y_prompt = rmsnorm(xp, final_norm)
    y_sample = rmsnorm(xs, final_norm)
    return (y_prompt, y_sample,
            jnp.stack(ret_p), jnp.stack(ret_s), jnp.stack(gla_p), jnp.stack(gla_s),
            jnp.stack(kr_p), jnp.stack(vr_p), jnp.stack(kr_s), jnp.stack(vr_s),
            jnp.stack(mk_p), jnp.stack(mv_p))
```

```python
import functools
import math

import numpy as np
import jax
import jax.numpy as jnp
from jax import lax
from jax.experimental import pallas as pl
from jax.experimental.pallas import tpu as pltpu

F32 = jnp.float32
BF16 = jnp.bfloat16

D_MODEL = 1024
D_FF = 2816
EPS = 1e-6
CHUNK = 64
H_A, DK_A, DV_A = 4, 64, 128
GLA_RANK = 16
GLA_TAU = 16.0
H_C, DH_C = 8, 64
H_X, DH_X = 4, 256
N_MEM = 256
PAGE = 128
NEG = -1e30

LANE = 128
FF_CHUNK = 256
MIB = 1024 * 1024

Z_RQ, Z_RK, Z_RV, Z_RG, Z_GQ, Z_GK, Z_GV, Z_GG, Z_GLR = 0, 256, 512, 1024, 1536, 1792, 2048, 2560, 3072
Z_W = 3200

_NT = (((1,), (1,)), ((), ()))
_TN = (((0,), (0,)), ((), ()))


def _params(sem, vmem_mib):
    return pltpu.CompilerParams(dimension_semantics=sem, vmem_limit_bytes=vmem_mib * MIB)


def _resident(shape):
    nd = len(shape)
    return pl.BlockSpec(shape, lambda *_: (0,) * nd, pipeline_mode=pl.Buffered(1))


def _rms(x, g):
    return x * lax.rsqrt(jnp.mean(x * x, axis=-1, keepdims=True) + EPS) * g


def _head_rms(o):
    return o * lax.rsqrt(jnp.mean(o * o, axis=-1, keepdims=True) + EPS)


def _silu(x):
    return x * jax.nn.sigmoid(x)


def _dot(a, b):
    return jnp.dot(a, b, preferred_element_type=F32)


def _ffn_kernel(x_ref, g_ref, wg_ref, wu_ref, wd_ref, *rest, final):
    if final:
        fg_ref, o_ref, a_ref = rest
    else:
        o_ref, a_ref = rest
    x = x_ref[...]
    h = _rms(x, g_ref[...]).astype(BF16)
    for c in range(D_FF // FF_CHUNK):
        sl = slice(c * FF_CHUNK, (c + 1) * FF_CHUNK)
        g = _dot(h, wg_ref[:, sl])
        u = _dot(h, wu_ref[:, sl])
        a_ref[:, sl] = (_silu(g) * u).astype(BF16)
    y = x + 0.5 * _dot(a_ref[...], wd_ref[...])
    if final:
        y = _rms(y, fg_ref[...])
    o_ref[...] = y


def _ffn(x, gain, wg, wu, wd, final_gain=None):
    n = x.shape[0]
    tm = min(512, n)
    final = final_gain is not None
    row = pl.BlockSpec((tm, D_MODEL), lambda i: (i, 0))
    in_specs = [row, _resident((1, D_MODEL)), _resident((D_MODEL, D_FF)), _resident((D_MODEL, D_FF)),
                _resident((D_FF, D_MODEL))]
    args = [x, gain.reshape(1, D_MODEL), wg, wu, wd]
    if final:
        in_specs.append(_resident((1, D_MODEL)))
        args.append(final_gain.reshape(1, D_MODEL))
    return pl.pallas_call(
        functools.partial(_ffn_kernel, final=final),
        out_shape=jax.ShapeDtypeStruct((n, D_MODEL), F32),
        grid=(n // tm,), in_specs=in_specs, out_specs=row,
        scratch_shapes=[pltpu.VMEM((tm, D_FF), BF16)],
        compiler_params=_params(("arbitrary",), 48), name="ffn")(*args)


def _proj_kernel(x_ref, *rest, has_norm, splits):
    if has_norm:
        g_ref, w_ref, *o_refs = rest
        h = _rms(x_ref[...], g_ref[...]).astype(BF16)
    else:
        w_ref, *o_refs = rest
        h = x_ref[...].astype(BF16)
    off = 0
    for o_ref, n in zip(o_refs, splits):
        o_ref[...] = _dot(h, w_ref[:, off:off + n])
        off += n


def _proj(x, gain, w, splits):
    n, k = x.shape
    tm = min(512, n)
    has_norm = gain is not None
    in_specs = [pl.BlockSpec((tm, k), lambda i: (i, 0))]
    args = [x]
    if has_norm:
        in_specs.append(_resident((1, k)))
        args.append(gain.reshape(1, k))
    in_specs.append(_resident(w.shape))
    args.append(w)
    return pl.pallas_call(
        functools.partial(_proj_kernel, has_norm=has_norm, splits=splits),
        out_shape=[jax.ShapeDtypeStruct((n, s), F32) for s in splits],
        grid=(n // tm,), in_specs=in_specs,
        out_specs=[pl.BlockSpec((tm, s), lambda i: (i, 0)) for s in splits],
        compiler_params=_params(("arbitrary",), 40), name="proj")(*args)


def _outproj_kernel(a_ref, w_ref, r_ref, o_ref):
    o_ref[...] = r_ref[...] + _dot(a_ref[...], w_ref[...])


def _outproj(a, w, res):
    n, k = a.shape
    tm = min(512, n)
    return pl.pallas_call(
        _outproj_kernel, out_shape=jax.ShapeDtypeStruct((n, D_MODEL), F32), grid=(n // tm,),
        in_specs=[pl.BlockSpec((tm, k), lambda i: (i, 0)), _resident(w.shape),
                  pl.BlockSpec((tm, D_MODEL), lambda i: (i, 0))],
        out_specs=pl.BlockSpec((tm, D_MODEL), lambda i: (i, 0)),
        compiler_params=_params(("arbitrary",), 32), name="outproj")(a, w, res)


_LOG_GAMMA = [math.log1p(-2.0 ** (-5.0 - h)) for h in range(H_A)]


def _even_kernel(z_ref, sr0_ref, sg0_ref, wa2_ref, ba_ref, gr_ref, gg_ref, o_ref, sr_ref, sg_ref, st_ref,
                 *, n_chunks, valid):
    C = CHUNK
    step = pl.program_id(1)

    def pair_state(a, b):
        zero = jnp.zeros((DK_A, DV_A), F32)
        s = jnp.concatenate([jnp.concatenate([a, zero], axis=1), jnp.concatenate([zero, b], axis=1)], axis=0)
        return s.T

    @pl.when(step == 0)
    def _():
        for pp in range(2):
            st_ref[pp] = pair_state(sr0_ref[2 * pp], sr0_ref[2 * pp + 1])
            st_ref[2 + pp] = pair_state(sg0_ref[2 * pp], sg0_ref[2 * pp + 1])

    row = lax.broadcasted_iota(jnp.int32, (C, LANE), 0)
    lane = lax.broadcasted_iota(jnp.int32, (C, LANE), 1)
    lo = lane < DK_A
    r2 = lax.broadcasted_iota(jnp.int32, (2 * C, C), 0)
    c2 = lax.broadcasted_iota(jnp.int32, (2 * C, C), 1)
    causal2 = jnp.where(r2 >= C, r2 - C, r2) >= c2
    tr = lax.broadcasted_iota(jnp.int32, (C, C), 0)
    tc = lax.broadcasted_iota(jnp.int32, (C, C), 1)
    tril = (tr >= tc).astype(F32)
    sr_ = lax.broadcasted_iota(jnp.int32, (2 * DV_A, 2 * DK_A), 0)
    sc_ = lax.broadcasted_iota(jnp.int32, (2 * DV_A, 2 * DK_A), 1)
    blockdiag = (sr_ >= DV_A) == (sc_ >= DK_A)
    steps_done = jnp.minimum(row + 1, valid).astype(F32)

    for ci in range(n_chunks):
        rows = slice(ci * C, (ci + 1) * C)
        glr = z_ref[rows, Z_GLR:Z_GLR + LANE].astype(BF16)
        xg = _dot(glr, wa2_ref[...]) + ba_ref[...]
        la = (jnp.minimum(xg, 0.0) - jnp.log1p(jnp.exp(-jnp.abs(xg)))) / GLA_TAU
        if valid < C:
            la = jnp.where(lax.broadcasted_iota(jnp.int32, la.shape, 0) < valid, la, 0.0)
        b_gla = jnp.dot(tril, la, precision=lax.Precision.HIGHEST, preferred_element_type=F32)
        for p in range(4):
            pp = p % 2
            if p < 2:
                q = z_ref[rows, Z_RQ + LANE * pp:Z_RQ + LANE * (pp + 1)]
                k = z_ref[rows, Z_RK + LANE * pp:Z_RK + LANE * (pp + 1)] * DK_A ** -0.5
                v = z_ref[rows, Z_RV + 2 * LANE * pp:Z_RV + 2 * LANE * (pp + 1)]
                b = steps_done * jnp.where(lo, _LOG_GAMMA[2 * pp], _LOG_GAMMA[2 * pp + 1])
                z_gate, gain_ref, o_col = Z_RG, gr_ref, 0
            else:
                q = z_ref[rows, Z_GQ + LANE * pp:Z_GQ + LANE * (pp + 1)] * DK_A ** -0.5
                k = z_ref[rows, Z_GK + LANE * pp:Z_GK + LANE * (pp + 1)]
                v = z_ref[rows, Z_GV + 2 * LANE * pp:Z_GV + 2 * LANE * (pp + 1)]
                b = b_gla[:, LANE * pp:LANE * (pp + 1)]
                z_gate, gain_ref, o_col = Z_GG, gg_ref, H_A * DV_A
            b_last = b[C - 1:C, :]
            qd = q * jnp.exp(b)
            k_in = (k * jnp.exp(-b)).astype(BF16)
            k_out = (k * jnp.exp(b_last - b)).astype(BF16)
            vb = v.astype(BF16)
            q2 = jnp.concatenate([jnp.where(lo, qd, 0.0), jnp.where(lo, 0.0, qd)], axis=0).astype(BF16)
            att = lax.dot_general(q2, k_in, _NT, preferred_element_type=F32)
            att = jnp.where(causal2, att, 0.0).astype(BF16)
            st = st_ref[p]
            o = lax.dot_general(qd.astype(BF16), st.astype(BF16), _NT, preferred_element_type=F32)
            o = o + jnp.concatenate([_dot(att[:C], vb[:, :DV_A]), _dot(att[C:], vb[:, DV_A:])], axis=1)
            delta_t = lax.dot_general(vb, k_out, _TN, preferred_element_type=F32)
            st_ref[p] = st * jnp.exp(b_last) + jnp.where(blockdiag, delta_t, 0.0)
            for hh in range(2):
                head = 2 * pp + hh
                oh = _head_rms(o[:, DV_A * hh:DV_A * (hh + 1)])
                gate = z_ref[rows, z_gate + DV_A * head:z_gate + DV_A * (head + 1)]
                gain = gain_ref[:, DV_A * head:DV_A * (head + 1)]
                o_ref[rows, o_col + DV_A * head:o_col + DV_A * (head + 1)] = (oh * gain * _silu(gate)).astype(BF16)

    @pl.when(step == pl.num_programs(1) - 1)
    def _():
        for p in range(4):
            s = st_ref[p].T
            out = sr_ref if p < 2 else sg_ref
            out[2 * (p % 2)] = s[:DK_A, :DV_A]
            out[2 * (p % 2) + 1] = s[DK_A:, DV_A:]


def _even_mixer(z, s_ret, s_gla, wa2, ba, g_ret, g_gla, valid):
    bsz, L, _ = z.shape
    rows = min(L, 4 * CHUNK)
    state = pl.BlockSpec((None, H_A, DK_A, DV_A), lambda b, s: (b, 0, 0, 0))
    st_shape = jax.ShapeDtypeStruct((bsz, H_A, DK_A, DV_A), F32)
    return pl.pallas_call(
        functools.partial(_even_kernel, n_chunks=rows // CHUNK, valid=valid),
        out_shape=[jax.ShapeDtypeStruct((bsz, L, D_MODEL), BF16), st_shape, st_shape],
        grid=(bsz, L // rows),
        in_specs=[pl.BlockSpec((None, rows, Z_W), lambda b, s: (b, s, 0)), state, state,
                  _resident((LANE, 2 * LANE)), _resident((1, 2 * LANE)),
                  _resident((1, H_A * DV_A)), _resident((1, H_A * DV_A))],
        out_specs=[pl.BlockSpec((None, rows, D_MODEL), lambda b, s: (b, s, 0)), state, state],
        scratch_shapes=[pltpu.VMEM((4, 2 * DV_A, 2 * DK_A), F32)],
        compiler_params=_params(("arbitrary", "arbitrary"), 32), name="even_mixer")(
            z, s_ret, s_gla, wa2, ba, g_ret, g_gla)


def _diff_lambda(lq1_ref, lk1_ref, lq2_ref, lk2_ref, lam_init):
    a = jnp.exp(jnp.sum(lq1_ref[...] * lk1_ref[...], axis=-1, keepdims=True))
    b = jnp.exp(jnp.sum(lq2_ref[...] * lk2_ref[...], axis=-1, keepdims=True))
    return a - b + lam_init


def _two_maps(q):
    lane = lax.broadcasted_iota(jnp.int32, q.shape, 1)
    lo = lane < DH_C
    return jnp.concatenate([jnp.where(lo, q, 0.0), jnp.where(lo, 0.0, q)], axis=0)


def _softmax_step(s, vb, m_ref, l_ref, acc_ref):
    m_old = m_ref[...]
    m_new = jnp.maximum(m_old, jnp.max(s, axis=-1, keepdims=True))
    corr = jnp.exp(m_old - m_new)
    p = jnp.exp(s - m_new)
    l_ref[...] = l_ref[...] * corr + jnp.sum(p, axis=-1, keepdims=True)
    acc_ref[...] = acc_ref[...] * corr + _dot(p.astype(BF16), vb)
    m_ref[...] = m_new


def _dprompt_kernel(slope_ref, q_ref, k_ref, v_ref, d0_ref, lq1_ref, lk1_ref, lq2_ref, lk2_ref, gain_ref,
                    o_ref, m_ref, l_ref, acc_ref, *, tq, lam_init):
    h = pl.program_id(1)
    qi = pl.program_id(2)
    slope = slope_ref[h]
    q2 = _two_maps(q_ref[...] * DH_C ** -0.5).astype(BF16)
    m_ref[...] = jnp.full_like(m_ref, NEG)
    l_ref[...] = jnp.zeros_like(l_ref)
    acc_ref[...] = jnp.zeros_like(acc_ref)

    def body(kb, carry):
        start = pl.multiple_of(kb * tq, tq)
        kblk = k_ref[pl.ds(start, tq), :].astype(BF16)
        vblk = v_ref[pl.ds(start, tq), :].astype(BF16)
        s = lax.dot_general(q2, kblk, _NT, preferred_element_type=F32)
        dist = d0_ref[...] + ((qi - kb) * tq).astype(F32)
        s = s + jnp.where(dist >= 0.0, -slope * dist, NEG)
        _softmax_step(s, vblk, m_ref, l_ref, acc_ref)
        return carry

    lax.fori_loop(0, qi + 1, body, 0)
    o = acc_ref[...] / l_ref[...]
    lam = _diff_lambda(lq1_ref, lk1_ref, lq2_ref, lk2_ref, lam_init)
    od = o[:tq] - lam * o[tq:]
    o_ref[...] = (_head_rms(od) * gain_ref[...] * (1.0 - lam_init)).astype(BF16)


def _diff_prompt(q, k, v, slopes, lams, gain, lam_init):
    bsz, L, _ = q.shape
    tq = 256
    hd = 2 * DH_C
    idx = np.arange(2 * tq)[:, None] % tq - np.arange(tq)[None, :]
    d0 = jnp.asarray(idx, F32)
    kv = pl.BlockSpec((None, L, hd), lambda b, h, i: (b, 0, h))
    vec = lambda n: pl.BlockSpec((1, n), lambda b, h, i: (0, 0))
    return pl.pallas_call(
        functools.partial(_dprompt_kernel, tq=tq, lam_init=lam_init),
        out_shape=jax.ShapeDtypeStruct((bsz, L, D_MODEL), BF16),
        grid=(bsz, H_C, L // tq),
        in_specs=[pl.BlockSpec(memory_space=pltpu.SMEM),
                  pl.BlockSpec((None, tq, hd), lambda b, h, i: (b, i, h)), kv, kv,
                  pl.BlockSpec((2 * tq, tq), lambda b, h, i: (0, 0)),
                  vec(DH_C), vec(DH_C), vec(DH_C), vec(DH_C), vec(hd)],
        out_specs=pl.BlockSpec((None, tq, hd), lambda b, h, i: (b, i, h)),
        scratch_shapes=[pltpu.VMEM((2 * tq, 1), F32), pltpu.VMEM((2 * tq, 1), F32), pltpu.VMEM((2 * tq, hd), F32)],
        compiler_params=_params(("arbitrary", "arbitrary", "arbitrary"), 32), name="diff_prompt")(
            slopes, q, k, v, d0, *lams, gain)


PAGES_PER_STEP = 8


def _dsample_kernel(pt_ref, q_ref, kn_ref, vn_ref, t1_ref, t1n_ref, slope_ref, lq1_ref, lk1_ref, lq2_ref, lk2_ref,
                    gain_ref, *rest, lam_init):
    k_refs = rest[:PAGES_PER_STEP]
    v_refs = rest[PAGES_PER_STEP:2 * PAGES_PER_STEP]
    o_ref, qall_ref, m_ref, l_ref, acc_ref = rest[2 * PAGES_PER_STEP:]
    j = pl.program_id(1)
    T = q_ref.shape[0]
    hd = 2 * DH_C

    @pl.when(j == 0)
    def _():
        for h in range(H_C):
            qall_ref[2 * T * h:2 * T * (h + 1), :] = _two_maps(q_ref[:, hd * h:hd * (h + 1)] * DH_C ** -0.5).astype(BF16)
        m_ref[...] = jnp.full_like(m_ref, NEG)
        l_ref[...] = jnp.zeros_like(l_ref)
        acc_ref[...] = jnp.zeros_like(acc_ref)

    qall = qall_ref[...]
    for i in range(PAGES_PER_STEP):
        k2 = k_refs[i][...].reshape(PAGE * H_C, hd).astype(BF16)
        v2 = v_refs[i][...].reshape(PAGE * H_C, hd).astype(BF16)
        s = lax.dot_general(qall, k2, _NT, preferred_element_type=F32)
        page_pos = ((j * PAGES_PER_STEP + i) * PAGE).astype(F32)
        s = s + (t1_ref[...] + slope_ref[...] * page_pos)
        _softmax_step(s, v2, m_ref, l_ref, acc_ref)

    @pl.when(j == pl.num_programs(1) - 1)
    def _():
        s = lax.dot_general(qall, kn_ref[...].astype(BF16), _NT, preferred_element_type=F32) + t1n_ref[...]
        _softmax_step(s, vn_ref[...].astype(BF16), m_ref, l_ref, acc_ref)
        o = acc_ref[...] / l_ref[...]
        lam = _diff_lambda(lq1_ref, lk1_ref, lq2_ref, lk2_ref, lam_init)
        for h in range(H_C):
            od = o[2 * T * h:2 * T * h + T] - lam * o[2 * T * h + T:2 * T * (h + 1)]
            o_ref[:, hd * h:hd * (h + 1)] = (_head_rms(od) * gain_ref[...] * (1.0 - lam_init)).astype(BF16)


def _diff_sample(q, k_new, v_new, cache_k, cache_v, layer, page_table, lams, gain, lam_init):
    bsz, T, _ = q.shape
    n_pages = page_table.shape[1]
    hd = 2 * DH_C
    n_rows = 2 * T * H_C
    assert n_rows == LANE and n_pages % PAGES_PER_STEP == 0
    c = np.arange(n_rows)
    c_h, c_t = c // (2 * T), c % T
    slope_c = 2.0 ** (-8.0 * (c_h + 1) / H_C)
    qpos = n_pages * PAGE + c_t
    r = np.arange(PAGE * H_C)
    r_key, r_h = r // H_C, r % H_C
    same_head = r_h[None, :] == c_h[:, None]
    t1 = np.where(same_head, slope_c[:, None] * (r_key[None, :] - qpos[:, None]), NEG)
    new_ok = same_head[:, :LANE] & (r_key[None, :LANE] <= c_t[:, None]) & (r_key[None, :LANE] < T)
    t1n = np.where(new_ok, slope_c[:, None] * (r_key[None, :LANE] - c_t[:, None]), NEG)
    pad = lambda a: jnp.pad(a.reshape(bsz, T * H_C, hd), ((0, 0), (0, LANE - T * H_C), (0, 0)))

    def page_spec(i):
        return pl.BlockSpec((None, None, PAGE, H_C, hd),
                            lambda b, j, pt: (layer, pt[b, j * PAGES_PER_STEP + i], 0, 0, 0))

    const = lambda shape: pl.BlockSpec(shape, lambda b, j, pt: (0,) * len(shape))
    per_b = lambda shape: pl.BlockSpec((None,) + shape, lambda b, j, pt: (b,) + (0,) * len(shape))
    grid_spec = pltpu.PrefetchScalarGridSpec(
        num_scalar_prefetch=1, grid=(bsz, n_pages // PAGES_PER_STEP),
        in_specs=[per_b((T, D_MODEL)), per_b((LANE, hd)), per_b((LANE, hd)),
                  const((n_rows, PAGE * H_C)), const((n_rows, LANE)), const((n_rows, 1)),
                  const((1, DH_C)), const((1, DH_C)), const((1, DH_C)), const((1, DH_C)), const((1, hd))]
                 + [page_spec(i) for i in range(PAGES_PER_STEP)] * 2,
        out_specs=per_b((T, D_MODEL)),
        scratch_shapes=[pltpu.VMEM((n_rows, hd), BF16), pltpu.VMEM((n_rows, 1), F32), pltpu.VMEM((n_rows, 1), F32),
                        pltpu.VMEM((n_rows, hd), F32)])
    return pl.pallas_call(
        functools.partial(_dsample_kernel, lam_init=lam_init),
        out_shape=jax.ShapeDtypeStruct((bsz, T, D_MODEL), BF16), grid_spec=grid_spec,
        compiler_params=_params(("arbitrary", "arbitrary"), 40), name="diff_sample")(
            page_table, q, pad(k_new), pad(v_new), jnp.asarray(t1, F32), jnp.asarray(t1n, F32),
            jnp.asarray(slope_c[:, None], F32), *lams, gain,
            *([cache_k] * PAGES_PER_STEP), *([cache_v] * PAGES_PER_STEP))


def _cross_kernel(x_ref, g_ref, wq_ref, mk_ref, mv_ref, wo_ref, o_ref):
    G, T, _ = x_ref.shape
    x = x_ref[...].reshape(G * T, D_MODEL)
    q = _dot(_rms(x, g_ref[...]).astype(BF16), wq_ref[...])
    rows = []
    for g in range(G):
        heads = []
        for h in range(H_X):
            cols = slice(DH_X * h, DH_X * (h + 1))
            qh = q[g * T:(g + 1) * T, cols].astype(BF16)
            s = lax.dot_general(qh, mk_ref[g, :, cols].astype(BF16), _NT, preferred_element_type=F32) * DH_X ** -0.5
            p = jnp.exp(s - jnp.max(s, axis=-1, keepdims=True))
            p = p / jnp.sum(p, axis=-1, keepdims=True)
            heads.append(_dot(p.astype(BF16), mv_ref[g, :, cols].astype(BF16)))
        rows.append(jnp.concatenate(heads, axis=1))
    o = jnp.concatenate(rows, axis=0).astype(BF16)
    o_ref[...] = (x + _dot(o, wo_ref[...])).reshape(G, T, D_MODEL)


def _cross(x, gain, wq, mk, mv, wo, group, rows):
    bsz, L, _ = x.shape
    xs = pl.BlockSpec((group, rows, D_MODEL), lambda b, t: (b, t, 0))
    mem = pl.BlockSpec((group, N_MEM, D_MODEL), lambda b, t: (b, 0, 0))
    return pl.pallas_call(
        _cross_kernel, out_shape=jax.ShapeDtypeStruct(x.shape, F32), grid=(bsz // group, L // rows),
        in_specs=[xs, _resident((1, D_MODEL)), _resident((D_MODEL, D_MODEL)), mem, mem,
                  _resident((D_MODEL, D_MODEL))],
        out_specs=xs, compiler_params=_params(("arbitrary", "arbitrary"), 48), name="cross")(
            x, gain.reshape(1, D_MODEL), wq, mk, mv, wo)


def kernel(x_prompt, x_sample, state_ret, state_gla, cache_k, cache_v, cache_mem_k, cache_mem_v, page_table, mem_prompt, norm_ffn1, ffn1_wg, ffn1_wu, ffn1_wd, norm_mix, w_in_even, gla_wa2, gla_ba, ret_gain, gla_gain, w_in_odd, lam_q1, lam_k1, lam_q2, lam_k2, diff_gain, w_mix_out, norm_x, x_wq, x_wk, x_wv, x_wo, norm_ffn2, ffn2_wg, ffn2_wu, ffn2_wd, final_norm):
    bp, lp, _ = x_prompt.shape
    bs, ls, _ = x_sample.shape
    depth = norm_ffn1.shape[0]
    bf = lambda w: w.astype(BF16)
    xp = x_prompt.reshape(bp * lp, D_MODEL)
    xs = x_sample.reshape(bs * ls, D_MODEL)
    mem = mem_prompt.reshape(bp * N_MEM, D_MODEL)
    slopes = jnp.asarray([2.0 ** (-8.0 * (h + 1) / H_C) for h in range(H_C)], F32)
    ret_p, ret_s, gla_p, gla_s, kr_p, vr_p, kr_s, vr_s, mk_p, mv_p = ([] for _ in range(10))

    for li in range(depth):
        w1 = (bf(ffn1_wg[li]), bf(ffn1_wu[li]), bf(ffn1_wd[li]))
        xp = _ffn(xp, norm_ffn1[li], *w1)
        xs = _ffn(xs, norm_ffn1[li], *w1)
        w_out = bf(w_mix_out[li])
        if li % 2 == 0:
            e = li // 2
            w_in = bf(jnp.pad(w_in_even[e], ((0, 0), (0, Z_W - w_in_even.shape[2]))))
            wa2 = bf(jnp.pad(gla_wa2[e], ((0, LANE - GLA_RANK), (0, 0))))
            prm = (wa2, gla_ba[e].reshape(1, -1), ret_gain[e].reshape(1, -1), gla_gain[e].reshape(1, -1))
            (zp,) = _proj(xp, norm_mix[li], w_in, (Z_W,))
            (zs,) = _proj(xs, norm_mix[li], w_in, (Z_W,))
            zero = jnp.zeros((bp, H_A, DK_A, DV_A), F32)
            op, sr, sg = _even_mixer(zp.reshape(bp, lp, Z_W), zero, zero, *prm, valid=CHUNK)
            ret_p.append(sr); gla_p.append(sg)
            zs = jnp.pad(zs.reshape(bs, ls, Z_W), ((0, 0), (0, CHUNK - ls), (0, 0)))
            os_, sr, sg = _even_mixer(zs, state_ret[e], state_gla[e], *prm, valid=ls)
            ret_s.append(sr); gla_s.append(sg)
            os_ = os_[:, :ls]
        else:
            o = li // 2
            lam_init = 0.8 - 0.6 * math.exp(-0.3 * li)
            lams = tuple(a[o].reshape(1, DH_C) for a in (lam_q1, lam_k1, lam_q2, lam_k2))
            gain = diff_gain[o].reshape(1, 2 * DH_C)
            w_in = bf(w_in_odd[o])
            splits = (D_MODEL,) * 3
            qp, kp, vp = _proj(xp, norm_mix[li], w_in, splits)
            qs, ks, vs = _proj(xs, norm_mix[li], w_in, splits)
            shp = lambda a: a.reshape(bp, lp, D_MODEL)
            shs = lambda a: a.reshape(bs, ls, D_MODEL)
            op = _diff_prompt(shp(qp), shp(kp), shp(vp), slopes, lams, gain, lam_init)
            os_ = _diff_sample(shs(qs), shs(ks), shs(vs), cache_k, cache_v, o, page_table, lams, gain, lam_init)
            kr_p.append(kp.reshape(bp, lp, H_C, 2 * DH_C)); vr_p.append(vp.reshape(bp, lp, H_C, 2 * DH_C))
            kr_s.append(ks.reshape(bs, ls, H_C, 2 * DH_C)); vr_s.append(vs.reshape(bs, ls, H_C, 2 * DH_C))
        xp = _outproj(op.reshape(bp * lp, D_MODEL), w_out, xp)
        xs = _outproj(os_.reshape(bs * ls, D_MODEL), w_out, xs)
        mk, mv = _proj(mem, None, bf(jnp.concatenate([x_wk[li], x_wv[li]], axis=1)), (D_MODEL, D_MODEL))
        mk_p.append(mk.reshape(bp, N_MEM, H_X, DH_X)); mv_p.append(mv.reshape(bp, N_MEM, H_X, DH_X))
        wq, wo = bf(x_wq[li]), bf(x_wo[li])
        xp = _cross(xp.reshape(bp, lp, D_MODEL), norm_x[li], wq, mk.reshape(bp, N_MEM, D_MODEL),
                    mv.reshape(bp, N_MEM, D_MODEL), wo, group=1, rows=512).reshape(bp * lp, D_MODEL)
        xs = _cross(xs.reshape(bs, ls, D_MODEL), norm_x[li], wq, cache_mem_k[li].reshape(bs, N_MEM, D_MODEL),
                    cache_mem_v[li].reshape(bs, N_MEM, D_MODEL), wo, group=4, rows=ls).reshape(bs * ls, D_MODEL)
        w2 = (bf(ffn2_wg[li]), bf(ffn2_wu[li]), bf(ffn2_wd[li]))
        fg = final_norm if li == depth - 1 else None
        xp = _ffn(xp, norm_ffn2[li], *w2, final_gain=fg)
        xs = _ffn(xs, norm_ffn2[li], *w2, final_gain=fg)

    return (xp.reshape(bp, lp, D_MODEL), xs.reshape(bs, ls, D_MODEL),
            jnp.stack(ret_p), jnp.stack(ret_s), jnp.stack(gla_p), jnp.stack(gla_s),
            jnp.stack(kr_p), jnp.stack(vr_p), jnp.stack(kr_s), jnp.stack(vr_s),
            jnp.stack(mk_p), jnp.stack(mv_p))
```

```python
import functools
import math

import numpy as np
import jax
import jax.numpy as jnp
from jax import lax
from jax.experimental import pallas as pl
from jax.experimental.pallas import tpu as pltpu

F32 = jnp.float32
BF16 = jnp.bfloat16

D_MODEL = 1024
D_FF = 2816
EPS = 1e-6
CHUNK = 64
H_A, DK_A, DV_A = 4, 64, 128
GLA_RANK = 16
GLA_TAU = 16.0
H_C, DH_C = 8, 64
H_X, DH_X = 4, 256
N_MEM = 256
PAGE = 128
NEG = -1e30

LANE = 128
FF_CHUNK = 256
MIB = 1024 * 1024

Z_RQ, Z_RK, Z_RV, Z_RG, Z_GQ, Z_GK, Z_GV, Z_GG, Z_GLR = 0, 256, 512, 1024, 1536, 1792, 2048, 2560, 3072
Z_W = 3200

_NT = (((1,), (1,)), ((), ()))
_TN = (((0,), (0,)), ((), ()))


def _params(sem, vmem_mib):
    return pltpu.CompilerParams(dimension_semantics=sem, vmem_limit_bytes=vmem_mib * MIB)


def _resident(shape):
    nd = len(shape)
    return pl.BlockSpec(shape, lambda *_: (0,) * nd, pipeline_mode=pl.Buffered(1))


def _rms(x, g):
    return x * lax.rsqrt(jnp.mean(x * x, axis=-1, keepdims=True) + EPS) * g


def _head_rms(o):
    return o * lax.rsqrt(jnp.mean(o * o, axis=-1, keepdims=True) + EPS)


def _silu(x):
    return x * jax.nn.sigmoid(x)


def _dot(a, b):
    return jnp.dot(a, b, preferred_element_type=F32)


def _ffn_kernel(x_ref, g_ref, wg_ref, wu_ref, wd_ref, *rest, final):
    if final:
        fg_ref, o_ref, a_ref = rest
    else:
        o_ref, a_ref = rest
    x = x_ref[...]
    h = _rms(x, g_ref[...]).astype(BF16)
    for c in range(D_FF // FF_CHUNK):
        sl = slice(c * FF_CHUNK, (c + 1) * FF_CHUNK)
        g = _dot(h, wg_ref[:, sl])
        u = _dot(h, wu_ref[:, sl])
        a_ref[:, sl] = (_silu(g) * u).astype(BF16)
    y = x + 0.5 * _dot(a_ref[...], wd_ref[...])
    if final:
        y = _rms(y, fg_ref[...])
    o_ref[...] = y


def _ffn(x, gain, wg, wu, wd, final_gain=None):
    n = x.shape[0]
    tm = min(512, n)
    final = final_gain is not None
    row = pl.BlockSpec((tm, D_MODEL), lambda i: (i, 0))
    in_specs = [row, _resident((1, D_MODEL)), _resident((D_MODEL, D_FF)), _resident((D_MODEL, D_FF)),
                _resident((D_FF, D_MODEL))]
    args = [x, gain.reshape(1, D_MODEL), wg, wu, wd]
    if final:
        in_specs.append(_resident((1, D_MODEL)))
        args.append(final_gain.reshape(1, D_MODEL))
    return pl.pallas_call(
        functools.partial(_ffn_kernel, final=final),
        out_shape=jax.ShapeDtypeStruct((n, D_MODEL), F32),
        grid=(n // tm,), in_specs=in_specs, out_specs=row,
        scratch_shapes=[pltpu.VMEM((tm, D_FF), BF16)],
        compiler_params=_params(("arbitrary",), 48), name="ffn")(*args)


def _proj_kernel(x_ref, *rest, has_norm, outs):
    if has_norm:
        g_ref, w_ref, *o_refs = rest
        h = _rms(x_ref[...], g_ref[...]).astype(BF16)
    else:
        w_ref, *o_refs = rest
        h = x_ref[...].astype(BF16)
    done = {}
    for o_ref, (off, n, kind) in zip(o_refs, outs):
        if (off, n) not in done:
            done[(off, n)] = _dot(h, w_ref[:, off:off + n])
        y = done[(off, n)]
        if isinstance(kind, int):
            for hh in range(n // kind):
                o_ref[:, hh, :] = y[:, hh * kind:(hh + 1) * kind]
        else:
            o_ref[...] = y.astype(o_ref.dtype)


def _proj(x, gain, w, outs):
    n, k = x.shape
    tm = min(512, n)
    has_norm = gain is not None
    in_specs = [pl.BlockSpec((tm, k), lambda i: (i, 0))]
    args = [x]
    if has_norm:
        in_specs.append(_resident((1, k)))
        args.append(gain.reshape(1, k))
    in_specs.append(_resident(w.shape))
    args.append(w)
    out_shape, out_specs = [], []
    for _, width, kind in outs:
        if isinstance(kind, int):
            out_shape.append(jax.ShapeDtypeStruct((n, width // kind, kind), F32))
            out_specs.append(pl.BlockSpec((tm, width // kind, kind), lambda i: (i, 0, 0)))
        else:
            out_shape.append(jax.ShapeDtypeStruct((n, width), kind))
            out_specs.append(pl.BlockSpec((tm, width), lambda i: (i, 0)))
    return pl.pallas_call(
        functools.partial(_proj_kernel, has_norm=has_norm, outs=outs),
        out_shape=out_shape, grid=(n // tm,), in_specs=in_specs, out_specs=out_specs,
        compiler_params=_params(("arbitrary",), 40), name="proj")(*args)


def _outproj_kernel(a_ref, w_ref, r_ref, o_ref):
    o_ref[...] = r_ref[...] + _dot(a_ref[...], w_ref[...])


def _outproj(a, w, res):
    n, k = a.shape
    tm = min(512, n)
    return pl.pallas_call(
        _outproj_kernel, out_shape=jax.ShapeDtypeStruct((n, D_MODEL), F32), grid=(n // tm,),
        in_specs=[pl.BlockSpec((tm, k), lambda i: (i, 0)), _resident(w.shape),
                  pl.BlockSpec((tm, D_MODEL), lambda i: (i, 0))],
        out_specs=pl.BlockSpec((tm, D_MODEL), lambda i: (i, 0)),
        compiler_params=_params(("arbitrary",), 32), name="outproj")(a, w, res)


_LOG_GAMMA = [math.log1p(-2.0 ** (-5.0 - h)) for h in range(H_A)]


def _even_kernel(z_ref, sr0_ref, sg0_ref, wa2_ref, ba_ref, gr_ref, gg_ref, o_ref, sr_ref, sg_ref, st_ref,
                 *, n_chunks, valid):
    C = CHUNK
    step = pl.program_id(1)

    def pair_state(a, b):
        zero = jnp.zeros((DK_A, DV_A), F32)
        s = jnp.concatenate([jnp.concatenate([a, zero], axis=1), jnp.concatenate([zero, b], axis=1)], axis=0)
        return s.T

    @pl.when(step == 0)
    def _():
        for pp in range(2):
            st_ref[pp] = pair_state(sr0_ref[2 * pp], sr0_ref[2 * pp + 1])
            st_ref[2 + pp] = pair_state(sg0_ref[2 * pp], sg0_ref[2 * pp + 1])

    row = lax.broadcasted_iota(jnp.int32, (C, LANE), 0)
    lane = lax.broadcasted_iota(jnp.int32, (C, LANE), 1)
    lo = lane < DK_A
    r2 = lax.broadcasted_iota(jnp.int32, (2 * C, C), 0)
    c2 = lax.broadcasted_iota(jnp.int32, (2 * C, C), 1)
    causal2 = jnp.where(r2 >= C, r2 - C, r2) >= c2
    tr = lax.broadcasted_iota(jnp.int32, (C, C), 0)
    tc = lax.broadcasted_iota(jnp.int32, (C, C), 1)
    tril = (tr >= tc).astype(F32)
    sr_ = lax.broadcasted_iota(jnp.int32, (2 * DV_A, 2 * DK_A), 0)
    sc_ = lax.broadcasted_iota(jnp.int32, (2 * DV_A, 2 * DK_A), 1)
    blockdiag = (sr_ >= DV_A) == (sc_ >= DK_A)
    steps_done = jnp.minimum(row + 1, valid).astype(F32)

    for ci in range(n_chunks):
        rows = slice(ci * C, (ci + 1) * C)
        glr = z_ref[rows, Z_GLR:Z_GLR + LANE].astype(BF16)
        xg = _dot(glr, wa2_ref[...]) + ba_ref[...]
        la = (jnp.minimum(xg, 0.0) - jnp.log1p(jnp.exp(-jnp.abs(xg)))) / GLA_TAU
        if valid < C:
            la = jnp.where(lax.broadcasted_iota(jnp.int32, la.shape, 0) < valid, la, 0.0)
        b_gla = jnp.dot(tril, la, precision=lax.Precision.HIGHEST, preferred_element_type=F32)
        for p in range(4):
            pp = p % 2
            if p < 2:
                q = z_ref[rows, Z_RQ + LANE * pp:Z_RQ + LANE * (pp + 1)]
                k = z_ref[rows, Z_RK + LANE * pp:Z_RK + LANE * (pp + 1)] * DK_A ** -0.5
                v = z_ref[rows, Z_RV + 2 * LANE * pp:Z_RV + 2 * LANE * (pp + 1)]
                b = steps_done * jnp.where(lo, _LOG_GAMMA[2 * pp], _LOG_GAMMA[2 * pp + 1])
                z_gate, gain_ref, o_col = Z_RG, gr_ref, 0
            else:
                q = z_ref[rows, Z_GQ + LANE * pp:Z_GQ + LANE * (pp + 1)] * DK_A ** -0.5
                k = z_ref[rows, Z_GK + LANE * pp:Z_GK + LANE * (pp + 1)]
                v = z_ref[rows, Z_GV + 2 * LANE * pp:Z_GV + 2 * LANE * (pp + 1)]
                b = b_gla[:, LANE * pp:LANE * (pp + 1)]
                z_gate, gain_ref, o_col = Z_GG, gg_ref, H_A * DV_A
            b_last = b[C - 1:C, :]
            qd = q * jnp.exp(b)
            k_in = (k * jnp.exp(-b)).astype(BF16)
            k_out = (k * jnp.exp(b_last - b)).astype(BF16)
            vb = v.astype(BF16)
            q2 = jnp.concatenate([jnp.where(lo, qd, 0.0), jnp.where(lo, 0.0, qd)], axis=0).astype(BF16)
            att = lax.dot_general(q2, k_in, _NT, preferred_element_type=F32)
            att = jnp.where(causal2, att, 0.0).astype(BF16)
            st = st_ref[p]
            o = lax.dot_general(qd.astype(BF16), st.astype(BF16), _NT, preferred_element_type=F32)
            o = o + jnp.concatenate([_dot(att[:C], vb[:, :DV_A]), _dot(att[C:], vb[:, DV_A:])], axis=1)
            delta_t = lax.dot_general(vb, k_out, _TN, preferred_element_type=F32)
            st_ref[p] = st * jnp.exp(b_last) + jnp.where(blockdiag, delta_t, 0.0)
            for hh in range(2):
                head = 2 * pp + hh
                oh = _head_rms(o[:, DV_A * hh:DV_A * (hh + 1)])
                gate = z_ref[rows, z_gate + DV_A * head:z_gate + DV_A * (head + 1)]
                gain = gain_ref[:, DV_A * head:DV_A * (head + 1)]
                o_ref[rows, o_col + DV_A * head:o_col + DV_A * (head + 1)] = (oh * gain * _silu(gate)).astype(BF16)

    @pl.when(step == pl.num_programs(1) - 1)
    def _():
        for p in range(4):
            s = st_ref[p].T
            out = sr_ref if p < 2 else sg_ref
            out[2 * (p % 2)] = s[:DK_A, :DV_A]
            out[2 * (p % 2) + 1] = s[DK_A:, DV_A:]


def _even_mixer(z, s_ret, s_gla, wa2, ba, g_ret, g_gla, valid):
    bsz, L, _ = z.shape
    rows = min(L, 4 * CHUNK)
    state = pl.BlockSpec((None, H_A, DK_A, DV_A), lambda b, s: (b, 0, 0, 0))
    st_shape = jax.ShapeDtypeStruct((bsz, H_A, DK_A, DV_A), F32)
    return pl.pallas_call(
        functools.partial(_even_kernel, n_chunks=rows // CHUNK, valid=valid),
        out_shape=[jax.ShapeDtypeStruct((bsz, L, D_MODEL), BF16), st_shape, st_shape],
        grid=(bsz, L // rows),
        in_specs=[pl.BlockSpec((None, rows, Z_W), lambda b, s: (b, s, 0)), state, state,
                  _resident((LANE, 2 * LANE)), _resident((1, 2 * LANE)),
                  _resident((1, H_A * DV_A)), _resident((1, H_A * DV_A))],
        out_specs=[pl.BlockSpec((None, rows, D_MODEL), lambda b, s: (b, s, 0)), state, state],
        scratch_shapes=[pltpu.VMEM((4, 2 * DV_A, 2 * DK_A), F32)],
        compiler_params=_params(("arbitrary", "arbitrary"), 32), name="even_mixer")(
            z, s_ret, s_gla, wa2, ba, g_ret, g_gla)


def _diff_lambda(lq1_ref, lk1_ref, lq2_ref, lk2_ref, lam_init):
    a = jnp.exp(jnp.sum(lq1_ref[...] * lk1_ref[...], axis=-1, keepdims=True))
    b = jnp.exp(jnp.sum(lq2_ref[...] * lk2_ref[...], axis=-1, keepdims=True))
    return a - b + lam_init


def _two_maps(q):
    lane = lax.broadcasted_iota(jnp.int32, q.shape, 1)
    lo = lane < DH_C
    return jnp.concatenate([jnp.where(lo, q, 0.0), jnp.where(lo, 0.0, q)], axis=0)


def _dprompt_kernel(slope_ref, q_ref, k_ref, v_ref, kpos_ref, lq1_ref, lk1_ref, lq2_ref, lk2_ref, gain_ref,
                    o_ref, kaug_ref, vt_ref, s_ref, *, tq, lam_init):
    h = pl.program_id(1)
    qi = pl.program_id(2)
    nk = vt_ref.shape[0]

    @pl.when(qi == 0)
    def _():
        kaug_ref[:, :LANE] = k_ref[...]
        kaug_ref[:, LANE:] = kpos_ref[...]
        for j in range(nk):
            vt_ref[j] = v_ref[j * tq:(j + 1) * tq, :].astype(F32).T.astype(BF16)

    q2 = _two_maps(q_ref[...].astype(F32) * DH_C ** -0.5)
    lane = lax.broadcasted_iota(jnp.int32, q2.shape, 1)
    q2aug = jnp.concatenate([q2, jnp.where(lane < 2, slope_ref[h], 0.0)], axis=1).astype(BF16)
    key = lax.broadcasted_iota(jnp.int32, (tq, 2 * tq), 0)
    qry = lax.broadcasted_iota(jnp.int32, (tq, 2 * tq), 1)
    causal = key <= jnp.where(qry >= tq, qry - tq, qry)

    def attend(n):
        m = None
        for j in range(n):
            s = lax.dot_general(kaug_ref[j * tq:(j + 1) * tq, :], q2aug, _NT, preferred_element_type=F32)
            if j == n - 1:
                s = jnp.where(causal, s, NEG)
            s_ref[j] = s
            mj = jnp.max(s, axis=0, keepdims=True)
            m = mj if m is None else jnp.maximum(m, mj)
        l = None
        acc = None
        for j in range(n):
            p = jnp.exp(s_ref[j] - m)
            lj = jnp.sum(p, axis=0, keepdims=True)
            aj = _dot(vt_ref[j], p.astype(BF16))
            l = lj if l is None else l + lj
            acc = aj if acc is None else acc + aj
        o = acc / l
        lam = _diff_lambda(lq1_ref, lk1_ref, lq2_ref, lk2_ref, lam_init)
        od = (o[:, :tq] - lam * o[:, tq:]).T
        o_ref[...] = (_head_rms(od) * gain_ref[...] * (1.0 - lam_init)).astype(BF16)

    for n in range(1, nk + 1):
        pl.when(qi == n - 1)(functools.partial(attend, n))


def _diff_prompt(q, k, v, slopes, lams, gain, lam_init):
    bsz, L, _ = q.shape
    tq = 256
    hd = 2 * DH_C
    pos = np.arange(L)
    kpos = np.zeros((L, LANE), np.float32)
    kpos[:, 0] = pos // 16 * 16
    kpos[:, 1] = pos % 16
    assert L <= 4096 and L % tq == 0
    kv = pl.BlockSpec((None, L, hd), lambda b, h, i: (b, 0, h))
    vec = lambda n: pl.BlockSpec((1, n), lambda b, h, i: (0, 0))
    return pl.pallas_call(
        functools.partial(_dprompt_kernel, tq=tq, lam_init=lam_init),
        out_shape=jax.ShapeDtypeStruct((bsz, L, D_MODEL), BF16),
        grid=(bsz, H_C, L // tq),
        in_specs=[pl.BlockSpec(memory_space=pltpu.SMEM),
                  pl.BlockSpec((None, tq, hd), lambda b, h, i: (b, i, h)), kv, kv,
                  pl.BlockSpec((L, LANE), lambda b, h, i: (0, 0)),
                  vec(DH_C), vec(DH_C), vec(DH_C), vec(DH_C), vec(hd)],
        out_specs=pl.BlockSpec((None, tq, hd), lambda b, h, i: (b, i, h)),
        scratch_shapes=[pltpu.VMEM((L, 2 * LANE), BF16), pltpu.VMEM((L // tq, hd, tq), BF16),
                        pltpu.VMEM((L // tq, tq, 2 * tq), F32)],
        compiler_params=_params(("arbitrary", "arbitrary", "arbitrary"), 32), name="diff_prompt")(
            slopes, q, k, v, jnp.asarray(kpos, BF16), *lams, gain)


PAGES_PER_STEP = 8


def _row_max(s):
    m = s[:, :LANE]
    for j in range(1, s.shape[1] // LANE):
        m = jnp.maximum(m, s[:, j * LANE:(j + 1) * LANE])
    return jnp.broadcast_to(jnp.max(m, axis=1, keepdims=True), m.shape)


def _row_sum(p):
    return jnp.broadcast_to(jnp.sum(p, axis=1, keepdims=True), p.shape)


def _dsample_kernel(pt_ref, q_ref, kn_ref, vn_ref, t1_ref, t1n_ref, slope_ref, lq1_ref, lk1_ref, lq2_ref, lk2_ref,
                    gain_ref, *rest, lam_init):
    P = PAGES_PER_STEP
    k_refs = rest[:P]
    v_refs = rest[P:2 * P]
    o_ref, qall_ref, m_ref, l_ref, acc_ref, s_ref, p_ref = rest[2 * P:]
    j = pl.program_id(1)
    T = q_ref.shape[0]
    hd = 2 * DH_C
    n_blk = PAGE * H_C // LANE

    @pl.when(j == 0)
    def _():
        for h in range(H_C):
            qh = q_ref[:, hd * h:hd * (h + 1)].astype(F32) * DH_C ** -0.5
            qall_ref[2 * T * h:2 * T * (h + 1), :] = _two_maps(qh).astype(BF16)
        m_ref[...] = jnp.full_like(m_ref, NEG)
        l_ref[...] = jnp.zeros_like(l_ref)
        acc_ref[...] = jnp.zeros_like(acc_ref)

    qall = qall_ref[...]
    page_off = lambda i: slope_ref[...] * ((j * P + i) * PAGE).astype(F32)
    wide = 2 * LANE
    rm = None
    for i in range(P):
        pm = None
        for b in range(PAGE * H_C // wide):
            k2 = k_refs[i][b * wide // H_C:(b + 1) * wide // H_C].reshape(wide, hd).astype(BF16)
            s = lax.dot_general(qall, k2, _NT, preferred_element_type=F32) + t1_ref[:, b * wide:(b + 1) * wide]
            s_ref[i, :, b * wide:(b + 1) * wide] = s
            mb = jnp.maximum(s[:, :LANE], s[:, LANE:])
            pm = mb if pm is None else jnp.maximum(pm, mb)
        r = _row_max(pm) + page_off(i)
        rm = r if rm is None else jnp.maximum(rm, r)
    m_old = m_ref[...]
    m_new = jnp.maximum(m_old, rm)
    corr = jnp.exp(m_old - m_new)
    lsum = jnp.zeros((LANE, LANE), F32)
    pv = None
    for i in range(P):
        sub = m_new - page_off(i)
        for b in range(n_blk):
            p = jnp.exp(s_ref[i, :, b * LANE:(b + 1) * LANE] - sub)
            lsum = lsum + p
            p_ref[i, :, b * LANE:(b + 1) * LANE] = p.astype(BF16)
        v2 = v_refs[i][...].reshape(PAGE * H_C, hd).astype(BF16)
        d = _dot(p_ref[i], v2)
        pv = d if pv is None else pv + d
    l_ref[...] = l_ref[...] * corr + _row_sum(lsum)
    acc_ref[...] = acc_ref[...] * corr + pv
    m_ref[...] = m_new

    @pl.when(j == pl.num_programs(1) - 1)
    def _():
        s = lax.dot_general(qall, kn_ref[...], _NT, preferred_element_type=F32) + t1n_ref[...]
        m_prev = m_ref[...]
        m_fin = jnp.maximum(m_prev, _row_max(s))
        c = jnp.exp(m_prev - m_fin)
        p = jnp.exp(s - m_fin)
        l = l_ref[...] * c + _row_sum(p)
        o = (acc_ref[...] * c + _dot(p.astype(BF16), vn_ref[...])) / l
        lam = _diff_lambda(lq1_ref, lk1_ref, lq2_ref, lk2_ref, lam_init)
        for h in range(H_C):
            od = o[2 * T * h:2 * T * h + T] - lam * o[2 * T * h + T:2 * T * (h + 1)]
            o_ref[:, hd * h:hd * (h + 1)] = (_head_rms(od) * gain_ref[...] * (1.0 - lam_init)).astype(BF16)


def _diff_sample(q, k_new, v_new, cache_k, cache_v, layer, page_table, lams, gain, lam_init):
    bsz, T, _ = q.shape
    n_pages = page_table.shape[1]
    hd = 2 * DH_C
    P = PAGES_PER_STEP
    n_rows = 2 * T * H_C
    assert n_rows == LANE and n_pages % P == 0
    c = np.arange(n_rows)
    c_h, c_t = c // (2 * T), c % T
    slope_c = 2.0 ** (-8.0 * (c_h + 1) / H_C)
    qpos = n_pages * PAGE + c_t
    r = np.arange(PAGE * H_C)
    r_key, r_h = r // H_C, r % H_C
    same_head = r_h[None, :] == c_h[:, None]
    t1 = np.where(same_head, slope_c[:, None] * (r_key[None, :] - qpos[:, None]), NEG)
    new_ok = same_head[:, :LANE] & (r_key[None, :LANE] <= c_t[:, None]) & (r_key[None, :LANE] < T)
    t1n = np.where(new_ok, slope_c[:, None] * (r_key[None, :LANE] - c_t[:, None]), NEG)
    slope_tile = np.broadcast_to(slope_c[:, None], (n_rows, LANE))
    pad = lambda a: jnp.pad(a.reshape(bsz, T * H_C, hd), ((0, 0), (0, LANE - T * H_C), (0, 0)))

    def page_spec(i):
        return pl.BlockSpec((None, None, PAGE, H_C, hd),
                            lambda b, j, pt: (layer, pt[b, j * P + i], 0, 0, 0))

    const = lambda shape: pl.BlockSpec(shape, lambda b, j, pt: (0,) * len(shape))
    per_b = lambda shape: pl.BlockSpec((None,) + shape, lambda b, j, pt: (b,) + (0,) * len(shape))
    grid_spec = pltpu.PrefetchScalarGridSpec(
        num_scalar_prefetch=1, grid=(bsz, n_pages // P),
        in_specs=[per_b((T, D_MODEL)), per_b((LANE, hd)), per_b((LANE, hd)),
                  const((n_rows, PAGE * H_C)), const((n_rows, LANE)), const((n_rows, LANE)),
                  const((1, DH_C)), const((1, DH_C)), const((1, DH_C)), const((1, DH_C)), const((1, hd))]
                 + [page_spec(i) for i in range(P)] * 2,
        out_specs=per_b((T, D_MODEL)),
        scratch_shapes=[pltpu.VMEM((n_rows, hd), BF16), pltpu.VMEM((n_rows, LANE), F32),
                        pltpu.VMEM((n_rows, LANE), F32), pltpu.VMEM((n_rows, hd), F32),
                        pltpu.VMEM((P, n_rows, PAGE * H_C), F32), pltpu.VMEM((P, n_rows, PAGE * H_C), BF16)])
    return pl.pallas_call(
        functools.partial(_dsample_kernel, lam_init=lam_init),
        out_shape=jax.ShapeDtypeStruct((bsz, T, D_MODEL), BF16), grid_spec=grid_spec,
        compiler_params=_params(("arbitrary", "arbitrary"), 48), name="diff_sample")(
            page_table, q, pad(k_new), pad(v_new), jnp.asarray(t1, F32), jnp.asarray(t1n, F32),
            jnp.asarray(slope_tile, F32), *lams, gain, *([cache_k] * P), *([cache_v] * P))


def _cross_kernel(x_ref, g_ref, wq_ref, mk_ref, mv_ref, wo_ref, o_ref):
    G, T, _ = x_ref.shape
    x = x_ref[...].reshape(G * T, D_MODEL)
    q = _dot(_rms(x, g_ref[...]).astype(BF16), wq_ref[...])
    rows = []
    for g in range(G):
        heads = []
        for h in range(H_X):
            cols = slice(DH_X * h, DH_X * (h + 1))
            qh = q[g * T:(g + 1) * T, cols].astype(BF16)
            s = lax.dot_general(qh, mk_ref[g, :, cols].astype(BF16), _NT, preferred_element_type=F32) * DH_X ** -0.5
            p = jnp.exp(s - jnp.max(s, axis=-1, keepdims=True))
            p = p / jnp.sum(p, axis=-1, keepdims=True)
            heads.append(_dot(p.astype(BF16), mv_ref[g, :, cols].astype(BF16)))
        rows.append(jnp.concatenate(heads, axis=1))
    o = jnp.concatenate(rows, axis=0).astype(BF16)
    o_ref[...] = (x + _dot(o, wo_ref[...])).reshape(G, T, D_MODEL)


def _cross(x, gain, wq, mk, mv, wo, group, rows):
    bsz, L, _ = x.shape
    xs = pl.BlockSpec((group, rows, D_MODEL), lambda b, t: (b, t, 0))
    mem = pl.BlockSpec((group, N_MEM, D_MODEL), lambda b, t: (b, 0, 0))
    return pl.pallas_call(
        _cross_kernel, out_shape=jax.ShapeDtypeStruct(x.shape, F32), grid=(bsz // group, L // rows),
        in_specs=[xs, _resident((1, D_MODEL)), _resident((D_MODEL, D_MODEL)), mem, mem,
                  _resident((D_MODEL, D_MODEL))],
        out_specs=xs, compiler_params=_params(("arbitrary", "arbitrary"), 48), name="cross")(
            x, gain.reshape(1, D_MODEL), wq, mk, mv, wo)


def kernel(x_prompt, x_sample, state_ret, state_gla, cache_k, cache_v, cache_mem_k, cache_mem_v, page_table, mem_prompt, norm_ffn1, ffn1_wg, ffn1_wu, ffn1_wd, norm_mix, w_in_even, gla_wa2, gla_ba, ret_gain, gla_gain, w_in_odd, lam_q1, lam_k1, lam_q2, lam_k2, diff_gain, w_mix_out, norm_x, x_wq, x_wk, x_wv, x_wo, norm_ffn2, ffn2_wg, ffn2_wu, ffn2_wd, final_norm):
    bp, lp, _ = x_prompt.shape
    bs, ls, _ = x_sample.shape
    depth = norm_ffn1.shape[0]
    bf = lambda w: w.astype(BF16)
    xp = x_prompt.reshape(bp * lp, D_MODEL)
    xs = x_sample.reshape(bs * ls, D_MODEL)
    mem = mem_prompt.reshape(bp * N_MEM, D_MODEL)
    slopes = jnp.asarray([2.0 ** (-8.0 * (h + 1) / H_C) for h in range(H_C)], F32)
    ret_p, ret_s, gla_p, gla_s, kr_p, vr_p, kr_s, vr_s, mk_p, mv_p = ([] for _ in range(10))

    for li in range(depth):
        w1 = (bf(ffn1_wg[li]), bf(ffn1_wu[li]), bf(ffn1_wd[li]))
        xp = _ffn(xp, norm_ffn1[li], *w1)
        xs = _ffn(xs, norm_ffn1[li], *w1)
        w_out = bf(w_mix_out[li])
        if li % 2 == 0:
            e = li // 2
            w_in = bf(jnp.pad(w_in_even[e], ((0, 0), (0, Z_W - w_in_even.shape[2]))))
            wa2 = bf(jnp.pad(gla_wa2[e], ((0, LANE - GLA_RANK), (0, 0))))
            prm = (wa2, gla_ba[e].reshape(1, -1), ret_gain[e].reshape(1, -1), gla_gain[e].reshape(1, -1))
            (zp,) = _proj(xp, norm_mix[li], w_in, ((0, Z_W, F32),))
            (zs,) = _proj(xs, norm_mix[li], w_in, ((0, Z_W, F32),))
            zero = jnp.zeros((bp, H_A, DK_A, DV_A), F32)
            op, sr, sg = _even_mixer(zp.reshape(bp, lp, Z_W), zero, zero, *prm, valid=CHUNK)
            ret_p.append(sr); gla_p.append(sg)
            zs = jnp.pad(zs.reshape(bs, ls, Z_W), ((0, 0), (0, CHUNK - ls), (0, 0)))
            os_, sr, sg = _even_mixer(zs, state_ret[e], state_gla[e], *prm, valid=ls)
            ret_s.append(sr); gla_s.append(sg)
            os_ = os_[:, :ls]
        else:
            o = li // 2
            lam_init = 0.8 - 0.6 * math.exp(-0.3 * li)
            lams = tuple(a[o].reshape(1, DH_C) for a in (lam_q1, lam_k1, lam_q2, lam_k2))
            gain = diff_gain[o].reshape(1, 2 * DH_C)
            w_in = bf(w_in_odd[o])
            hd = 2 * DH_C
            outs = ((0, D_MODEL, BF16), (D_MODEL, D_MODEL, BF16), (2 * D_MODEL, D_MODEL, BF16),
                    (D_MODEL, D_MODEL, hd), (2 * D_MODEL, D_MODEL, hd))
            qp, kp, vp, kp4, vp4 = _proj(xp, norm_mix[li], w_in, outs)
            qs, ks, vs, ks4, vs4 = _proj(xs, norm_mix[li], w_in, outs)
            shp = lambda a: a.reshape(bp, lp, D_MODEL)
            shs = lambda a: a.reshape(bs, ls, D_MODEL)
            op = _diff_prompt(shp(qp), shp(kp), shp(vp), slopes, lams, gain, lam_init)
            os_ = _diff_sample(shs(qs), shs(ks), shs(vs), cache_k, cache_v, o, page_table, lams, gain, lam_init)
            kr_p.append(kp4.reshape(bp, lp, H_C, hd)); vr_p.append(vp4.reshape(bp, lp, H_C, hd))
            kr_s.append(ks4.reshape(bs, ls, H_C, hd)); vr_s.append(vs4.reshape(bs, ls, H_C, hd))
        xp = _outproj(op.reshape(bp * lp, D_MODEL), w_out, xp)
        xs = _outproj(os_.reshape(bs * ls, D_MODEL), w_out, xs)
        w_kv = bf(jnp.concatenate([x_wk[li], x_wv[li]], axis=1))
        mk, mv, mk4, mv4 = _proj(mem, None, w_kv, ((0, D_MODEL, BF16), (D_MODEL, D_MODEL, BF16),
                                                    (0, D_MODEL, DH_X), (D_MODEL, D_MODEL, DH_X)))
        mk_p.append(mk4.reshape(bp, N_MEM, H_X, DH_X)); mv_p.append(mv4.reshape(bp, N_MEM, H_X, DH_X))
        wq, wo = bf(x_wq[li]), bf(x_wo[li])
        xp = _cross(xp.reshape(bp, lp, D_MODEL), norm_x[li], wq, mk.reshape(bp, N_MEM, D_MODEL),
                    mv.reshape(bp, N_MEM, D_MODEL), wo, group=1, rows=512).reshape(bp * lp, D_MODEL)
        xs = _cross(xs.reshape(bs, ls, D_MODEL), norm_x[li], wq, cache_mem_k[li].reshape(bs, N_MEM, D_MODEL),
                    cache_mem_v[li].reshape(bs, N_MEM, D_MODEL), wo, group=4, rows=ls).reshape(bs * ls, D_MODEL)
        w2 = (bf(ffn2_wg[li]), bf(ffn2_wu[li]), bf(ffn2_wd[li]))
        fg = final_norm if li == depth - 1 else None
        xp = _ffn(xp, norm_ffn2[li], *w2, final_gain=fg)
        xs = _ffn(xs, norm_ffn2[li], *w2, final_gain=fg)

    return (xp.reshape(bp, lp, D_MODEL), xs.reshape(bs, ls, D_MODEL),
            jnp.stack(ret_p), jnp.stack(ret_s), jnp.stack(gla_p), jnp.stack(gla_s),
            jnp.stack(kr_p), jnp.stack(vr_p), jnp.stack(kr_s), jnp.stack(vr_s),
            jnp.stack(mk_p), jnp.stack(mv_p))
```

```python
import functools
import math

import numpy as np
import jax
import jax.numpy as jnp
from jax import lax
from jax.experimental import pallas as pl
from jax.experimental.pallas import tpu as pltpu

F32 = jnp.float32
BF16 = jnp.bfloat16

D_MODEL = 1024
D_FF = 2816
EPS = 1e-6
CHUNK = 64
H_A, DK_A, DV_A = 4, 64, 128
GLA_RANK = 16
GLA_TAU = 16.0
H_C, DH_C = 8, 64
H_X, DH_X = 4, 256
N_MEM = 256
PAGE = 128
NEG = -1e30

LANE = 128
FF_CHUNK = 256
MIB = 1024 * 1024

Z_RQ, Z_RK, Z_RV, Z_RG, Z_GQ, Z_GK, Z_GV, Z_GG, Z_GLR = 0, 256, 512, 1024, 1536, 1792, 2048, 2560, 3072
Z_W = 3200

_NT = (((1,), (1,)), ((), ()))
_TN = (((0,), (0,)), ((), ()))


def _params(sem, vmem_mib):
    return pltpu.CompilerParams(dimension_semantics=sem, vmem_limit_bytes=vmem_mib * MIB)


def _resident(shape):
    nd = len(shape)
    return pl.BlockSpec(shape, lambda *_: (0,) * nd, pipeline_mode=pl.Buffered(1))


def _rms(x, g):
    return x * lax.rsqrt(jnp.mean(x * x, axis=-1, keepdims=True) + EPS) * g


def _head_rms(o):
    return o * lax.rsqrt(jnp.mean(o * o, axis=-1, keepdims=True) + EPS)


def _silu(x):
    return x * jax.nn.sigmoid(x)


def _dot(a, b):
    return jnp.dot(a, b, preferred_element_type=F32)


def _ffn_kernel(x_ref, g_ref, wg_ref, wu_ref, wd_ref, *rest, final):
    if final:
        fg_ref, o_ref, a_ref = rest
    else:
        o_ref, a_ref = rest
    x = x_ref[...]
    h = _rms(x, g_ref[...]).astype(BF16)
    for c in range(D_FF // FF_CHUNK):
        sl = slice(c * FF_CHUNK, (c + 1) * FF_CHUNK)
        g = _dot(h, wg_ref[:, sl])
        u = _dot(h, wu_ref[:, sl])
        a_ref[:, sl] = (_silu(g) * u).astype(BF16)
    y = x + 0.5 * _dot(a_ref[...], wd_ref[...])
    if final:
        y = _rms(y, fg_ref[...])
    o_ref[...] = y


def _ffn(x, gain, wg, wu, wd, final_gain=None):
    n = x.shape[0]
    tm = min(512, n)
    final = final_gain is not None
    row = pl.BlockSpec((tm, D_MODEL), lambda i: (i, 0))
    in_specs = [row, _resident((1, D_MODEL)), _resident((D_MODEL, D_FF)), _resident((D_MODEL, D_FF)),
                _resident((D_FF, D_MODEL))]
    args = [x, gain.reshape(1, D_MODEL), wg, wu, wd]
    if final:
        in_specs.append(_resident((1, D_MODEL)))
        args.append(final_gain.reshape(1, D_MODEL))
    return pl.pallas_call(
        functools.partial(_ffn_kernel, final=final),
        out_shape=jax.ShapeDtypeStruct((n, D_MODEL), F32),
        grid=(n // tm,), in_specs=in_specs, out_specs=row,
        scratch_shapes=[pltpu.VMEM((tm, D_FF), BF16)],
        compiler_params=_params(("arbitrary",), 48), name="ffn")(*args)


def _proj_kernel(x_ref, *rest, has_norm, outs):
    if has_norm:
        g_ref, w_ref, *o_refs = rest
        h = _rms(x_ref[...], g_ref[...]).astype(BF16)
    else:
        w_ref, *o_refs = rest
        h = x_ref[...].astype(BF16)
    done = {}
    for o_ref, (off, n, kind) in zip(o_refs, outs):
        if (off, n) not in done:
            done[(off, n)] = _dot(h, w_ref[:, off:off + n])
        y = done[(off, n)]
        if isinstance(kind, int):
            for hh in range(n // kind):
                o_ref[:, hh, :] = y[:, hh * kind:(hh + 1) * kind]
        else:
            o_ref[...] = y.astype(o_ref.dtype)


def _proj(x, gain, w, outs):
    n, k = x.shape
    tm = min(512, n)
    has_norm = gain is not None
    in_specs = [pl.BlockSpec((tm, k), lambda i: (i, 0))]
    args = [x]
    if has_norm:
        in_specs.append(_resident((1, k)))
        args.append(gain.reshape(1, k))
    in_specs.append(_resident(w.shape))
    args.append(w)
    out_shape, out_specs = [], []
    for _, width, kind in outs:
        if isinstance(kind, int):
            out_shape.append(jax.ShapeDtypeStruct((n, width // kind, kind), F32))
            out_specs.append(pl.BlockSpec((tm, width // kind, kind), lambda i: (i, 0, 0)))
        else:
            out_shape.append(jax.ShapeDtypeStruct((n, width), kind))
            out_specs.append(pl.BlockSpec((tm, width), lambda i: (i, 0)))
    return pl.pallas_call(
        functools.partial(_proj_kernel, has_norm=has_norm, outs=outs),
        out_shape=out_shape, grid=(n // tm,), in_specs=in_specs, out_specs=out_specs,
        compiler_params=_params(("arbitrary",), 40), name="proj")(*args)


def _outproj_kernel(a_ref, w_ref, r_ref, o_ref):
    o_ref[...] = r_ref[...] + _dot(a_ref[...], w_ref[...])


def _outproj(a, w, res):
    n, k = a.shape
    tm = min(512, n)
    return pl.pallas_call(
        _outproj_kernel, out_shape=jax.ShapeDtypeStruct((n, D_MODEL), F32), grid=(n // tm,),
        in_specs=[pl.BlockSpec((tm, k), lambda i: (i, 0)), _resident(w.shape),
                  pl.BlockSpec((tm, D_MODEL), lambda i: (i, 0))],
        out_specs=pl.BlockSpec((tm, D_MODEL), lambda i: (i, 0)),
        compiler_params=_params(("arbitrary",), 32), name="outproj")(a, w, res)


_LOG_GAMMA = [math.log1p(-2.0 ** (-5.0 - h)) for h in range(H_A)]


def _even_kernel(z_ref, sr0_ref, sg0_ref, wa2_ref, ba_ref, gr_ref, gg_ref, o_ref, sr_ref, sg_ref, st_ref,
                 *, n_chunks, valid):
    C = CHUNK
    step = pl.program_id(1)

    def pair_state(a, b):
        zero = jnp.zeros((DK_A, DV_A), F32)
        s = jnp.concatenate([jnp.concatenate([a, zero], axis=1), jnp.concatenate([zero, b], axis=1)], axis=0)
        return s.T

    @pl.when(step == 0)
    def _():
        for pp in range(2):
            st_ref[pp] = pair_state(sr0_ref[2 * pp], sr0_ref[2 * pp + 1])
            st_ref[2 + pp] = pair_state(sg0_ref[2 * pp], sg0_ref[2 * pp + 1])

    row = lax.broadcasted_iota(jnp.int32, (C, LANE), 0)
    lane = lax.broadcasted_iota(jnp.int32, (C, LANE), 1)
    lo = lane < DK_A
    r2 = lax.broadcasted_iota(jnp.int32, (2 * C, C), 0)
    c2 = lax.broadcasted_iota(jnp.int32, (2 * C, C), 1)
    causal2 = jnp.where(r2 >= C, r2 - C, r2) >= c2
    tr = lax.broadcasted_iota(jnp.int32, (C, C), 0)
    tc = lax.broadcasted_iota(jnp.int32, (C, C), 1)
    tril = (tr >= tc).astype(F32)
    sr_ = lax.broadcasted_iota(jnp.int32, (2 * DV_A, 2 * DK_A), 0)
    sc_ = lax.broadcasted_iota(jnp.int32, (2 * DV_A, 2 * DK_A), 1)
    blockdiag = (sr_ >= DV_A) == (sc_ >= DK_A)
    steps_done = jnp.minimum(row + 1, valid).astype(F32)

    for ci in range(n_chunks):
        rows = slice(ci * C, (ci + 1) * C)
        glr = z_ref[rows, Z_GLR:Z_GLR + LANE].astype(BF16)
        xg = _dot(glr, wa2_ref[...]) + ba_ref[...]
        la = (jnp.minimum(xg, 0.0) - jnp.log1p(jnp.exp(-jnp.abs(xg)))) / GLA_TAU
        if valid < C:
            la = jnp.where(lax.broadcasted_iota(jnp.int32, la.shape, 0) < valid, la, 0.0)
        b_gla = jnp.dot(tril, la, precision=lax.Precision.HIGHEST, preferred_element_type=F32)
        for p in range(4):
            pp = p % 2
            if p < 2:
                q = z_ref[rows, Z_RQ + LANE * pp:Z_RQ + LANE * (pp + 1)]
                k = z_ref[rows, Z_RK + LANE * pp:Z_RK + LANE * (pp + 1)] * DK_A ** -0.5
                v = z_ref[rows, Z_RV + 2 * LANE * pp:Z_RV + 2 * LANE * (pp + 1)]
                b = steps_done * jnp.where(lo, _LOG_GAMMA[2 * pp], _LOG_GAMMA[2 * pp + 1])
                z_gate, gain_ref, o_col = Z_RG, gr_ref, 0
            else:
                q = z_ref[rows, Z_GQ + LANE * pp:Z_GQ + LANE * (pp + 1)] * DK_A ** -0.5
                k = z_ref[rows, Z_GK + LANE * pp:Z_GK + LANE * (pp + 1)]
                v = z_ref[rows, Z_GV + 2 * LANE * pp:Z_GV + 2 * LANE * (pp + 1)]
                b = b_gla[:, LANE * pp:LANE * (pp + 1)]
                z_gate, gain_ref, o_col = Z_GG, gg_ref, H_A * DV_A
            b_last = b[C - 1:C, :]
            qd = q * jnp.exp(b)
            k_in = (k * jnp.exp(-b)).astype(BF16)
            k_out = (k * jnp.exp(b_last - b)).astype(BF16)
            vb = v.astype(BF16)
            q2 = jnp.concatenate([jnp.where(lo, qd, 0.0), jnp.where(lo, 0.0, qd)], axis=0).astype(BF16)
            att = lax.dot_general(q2, k_in, _NT, preferred_element_type=F32)
            att = jnp.where(causal2, att, 0.0).astype(BF16)
            st = st_ref[p]
            o = lax.dot_general(qd.astype(BF16), st.astype(BF16), _NT, preferred_element_type=F32)
            o = o + jnp.concatenate([_dot(att[:C], vb[:, :DV_A]), _dot(att[C:], vb[:, DV_A:])], axis=1)
            delta_t = lax.dot_general(vb, k_out, _TN, preferred_element_type=F32)
            st_ref[p] = st * jnp.exp(b_last) + jnp.where(blockdiag, delta_t, 0.0)
            for hh in range(2):
                head = 2 * pp + hh
                oh = _head_rms(o[:, DV_A * hh:DV_A * (hh + 1)])
                gate = z_ref[rows, z_gate + DV_A * head:z_gate + DV_A * (head + 1)]
                gain = gain_ref[:, DV_A * head:DV_A * (head + 1)]
                o_ref[rows, o_col + DV_A * head:o_col + DV_A * (head + 1)] = (oh * gain * _silu(gate)).astype(BF16)

    @pl.when(step == pl.num_programs(1) - 1)
    def _():
        for p in range(4):
            s = st_ref[p].T
            out = sr_ref if p < 2 else sg_ref
            out[2 * (p % 2)] = s[:DK_A, :DV_A]
            out[2 * (p % 2) + 1] = s[DK_A:, DV_A:]


def _even_mixer(z, s_ret, s_gla, wa2, ba, g_ret, g_gla, valid):
    bsz, L, _ = z.shape
    rows = min(L, 4 * CHUNK)
    state = pl.BlockSpec((None, H_A, DK_A, DV_A), lambda b, s: (b, 0, 0, 0))
    st_shape = jax.ShapeDtypeStruct((bsz, H_A, DK_A, DV_A), F32)
    return pl.pallas_call(
        functools.partial(_even_kernel, n_chunks=rows // CHUNK, valid=valid),
        out_shape=[jax.ShapeDtypeStruct((bsz, L, D_MODEL), BF16), st_shape, st_shape],
        grid=(bsz, L // rows),
        in_specs=[pl.BlockSpec((None, rows, Z_W), lambda b, s: (b, s, 0)), state, state,
                  _resident((LANE, 2 * LANE)), _resident((1, 2 * LANE)),
                  _resident((1, H_A * DV_A)), _resident((1, H_A * DV_A))],
        out_specs=[pl.BlockSpec((None, rows, D_MODEL), lambda b, s: (b, s, 0)), state, state],
        scratch_shapes=[pltpu.VMEM((4, 2 * DV_A, 2 * DK_A), F32)],
        compiler_params=_params(("arbitrary", "arbitrary"), 32), name="even_mixer")(
            z, s_ret, s_gla, wa2, ba, g_ret, g_gla)


def _diff_lambda(lq1_ref, lk1_ref, lq2_ref, lk2_ref, lam_init):
    a = jnp.exp(jnp.sum(lq1_ref[...] * lk1_ref[...], axis=-1, keepdims=True))
    b = jnp.exp(jnp.sum(lq2_ref[...] * lk2_ref[...], axis=-1, keepdims=True))
    return a - b + lam_init


def _two_maps(q):
    lane = lax.broadcasted_iota(jnp.int32, q.shape, 1)
    lo = lane < DH_C
    return jnp.concatenate([jnp.where(lo, q, 0.0), jnp.where(lo, 0.0, q)], axis=0)


def _dprompt_kernel(slope_ref, qa_ref, qb_ref, k_ref, v_ref, kpos_ref, lq1_ref, lk1_ref, lq2_ref, lk2_ref, gain_ref,
                    oa_ref, ob_ref, kaug_ref, vt_ref, s_ref, *, tq, lam_init):
    h = pl.program_id(1)
    i = pl.program_id(2)
    nk = vt_ref.shape[0]

    @pl.when(i == 0)
    def _():
        kaug_ref[:, :LANE] = k_ref[...]
        kaug_ref[:, LANE:] = kpos_ref[...]
        for j in range(nk):
            vt_ref[j] = v_ref[j * tq:(j + 1) * tq, :].astype(F32).T.astype(BF16)

    key = lax.broadcasted_iota(jnp.int32, (tq, 2 * tq), 0)
    qry = lax.broadcasted_iota(jnp.int32, (tq, 2 * tq), 1)
    causal = key <= jnp.where(qry >= tq, qry - tq, qry)
    lam = _diff_lambda(lq1_ref, lk1_ref, lq2_ref, lk2_ref, lam_init)

    def attend(q_ref, o_ref, n, slot):
        q2 = _two_maps(q_ref[...].astype(F32) * DH_C ** -0.5)
        lane = lax.broadcasted_iota(jnp.int32, q2.shape, 1)
        q2aug = jnp.concatenate([q2, jnp.where(lane < 2, slope_ref[h], 0.0)], axis=1).astype(BF16)
        m = None
        for j in range(n):
            s = lax.dot_general(kaug_ref[j * tq:(j + 1) * tq, :], q2aug, _NT, preferred_element_type=F32)
            if j == n - 1:
                s = jnp.where(causal, s, NEG)
            s_ref[slot + j] = s
            mj = jnp.max(s, axis=0, keepdims=True)
            m = mj if m is None else jnp.maximum(m, mj)
        l = None
        acc = None
        for j in range(n):
            p = jnp.exp(s_ref[slot + j] - m)
            lj = jnp.sum(p, axis=0, keepdims=True)
            aj = _dot(vt_ref[j], p.astype(BF16))
            l = lj if l is None else l + lj
            acc = aj if acc is None else acc + aj
        o = acc / l
        od = (o[:, :tq] - lam * o[:, tq:]).T
        o_ref[...] = (_head_rms(od) * gain_ref[...] * (1.0 - lam_init)).astype(BF16)

    def pair(v):
        attend(qa_ref, oa_ref, v + 1, 0)
        attend(qb_ref, ob_ref, nk - v, v + 1)

    for v in range(nk // 2):
        pl.when(i == v)(functools.partial(pair, v))


def _diff_prompt(q, k, v, slopes, lams, gain, lam_init):
    bsz, L, _ = q.shape
    tq = 256
    nq = L // tq
    hd = 2 * DH_C
    pos = np.arange(L)
    kpos = np.zeros((L, LANE), np.float32)
    kpos[:, 0] = pos // 16 * 16
    kpos[:, 1] = pos % 16
    assert L <= 4096 and L % (2 * tq) == 0
    kv = pl.BlockSpec((None, L, hd), lambda b, h, i: (b, 0, h))
    vec = lambda n: pl.BlockSpec((1, n), lambda b, h, i: (0, 0))
    half = jax.ShapeDtypeStruct((bsz, L // 2, D_MODEL), BF16)
    lo, hi = pl.pallas_call(
        functools.partial(_dprompt_kernel, tq=tq, lam_init=lam_init),
        out_shape=[half, half],
        grid=(bsz, H_C, nq // 2),
        in_specs=[pl.BlockSpec(memory_space=pltpu.SMEM),
                  pl.BlockSpec((None, tq, hd), lambda b, h, i: (b, i, h)),
                  pl.BlockSpec((None, tq, hd), lambda b, h, i: (b, nq - 1 - i, h)), kv, kv,
                  pl.BlockSpec((L, LANE), lambda b, h, i: (0, 0)),
                  vec(DH_C), vec(DH_C), vec(DH_C), vec(DH_C), vec(hd)],
        out_specs=[pl.BlockSpec((None, tq, hd), lambda b, h, i: (b, i, h)),
                   pl.BlockSpec((None, tq, hd), lambda b, h, i: (b, nq // 2 - 1 - i, h))],
        scratch_shapes=[pltpu.VMEM((L, 2 * LANE), BF16), pltpu.VMEM((nq, hd, tq), BF16),
                        pltpu.VMEM((nq + 1, tq, 2 * tq), F32)],
        compiler_params=_params(("arbitrary", "arbitrary", "arbitrary"), 32), name="diff_prompt")(
            slopes, q, q, k, v, jnp.asarray(kpos, BF16), *lams, gain)
    return jnp.concatenate([lo, hi], axis=1)


PAGES_PER_STEP = 8


def _row_max(s):
    m = s[:, :LANE]
    for j in range(1, s.shape[1] // LANE):
        m = jnp.maximum(m, s[:, j * LANE:(j + 1) * LANE])
    return jnp.broadcast_to(jnp.max(m, axis=1, keepdims=True), m.shape)


def _row_sum(p):
    return jnp.broadcast_to(jnp.sum(p, axis=1, keepdims=True), p.shape)


def _dsample_kernel(pt_ref, q_ref, kn_ref, vn_ref, t1_ref, t1n_ref, slope_ref, lq1_ref, lk1_ref, lq2_ref,
                    lk2_ref, gain_ref, *rest, lam_init, n_pages):
    P = PAGES_PER_STEP
    k_refs = rest[:P]
    v_refs = rest[P:2 * P]
    o_ref, qall_ref, m_ref, corr_ref, l_ref, acc_ref, sa_ref, sb_ref, p_ref = rest[2 * P:]
    j = pl.program_id(1)
    n_steps = n_pages // P
    T = q_ref.shape[0]
    hd = 2 * DH_C
    n_blk = PAGE * H_C // LANE
    wide = 2 * LANE

    @pl.when(j == 0)
    def _():
        for h in range(H_C):
            qh = q_ref[:, hd * h:hd * (h + 1)].astype(F32) * DH_C ** -0.5
            qall_ref[2 * T * h:2 * T * (h + 1), :] = _two_maps(qh).astype(BF16)
        m_ref[...] = jnp.full_like(m_ref, NEG)
        corr_ref[...] = jnp.ones_like(corr_ref)
        l_ref[...] = jnp.zeros_like(l_ref)
        acc_ref[...] = jnp.zeros_like(acc_ref)
        sb_ref[...] = jnp.full_like(sb_ref, NEG)

    qall = qall_ref[...]
    page_off = lambda step, i: slope_ref[...] * ((step * P + i) * PAGE).astype(F32)

    def body(s_w, s_r):
        m_prev = m_ref[...]
        corr_prev = corr_ref[...]
        step = jnp.minimum(j, n_steps - 1)
        rm = None
        for i in range(P):
            pm = None
            for b in range(PAGE * H_C // wide):
                k2 = k_refs[i][b * wide // H_C:(b + 1) * wide // H_C].reshape(wide, hd).astype(BF16)
                s = lax.dot_general(qall, k2, _NT, preferred_element_type=F32) + t1_ref[:, b * wide:(b + 1) * wide]
                s_w[i, :, b * wide:(b + 1) * wide] = s
                mb = jnp.maximum(s[:, :LANE], s[:, LANE:])
                pm = mb if pm is None else jnp.maximum(pm, mb)
            r = _row_max(pm) + page_off(step, i)
            rm = r if rm is None else jnp.maximum(rm, r)
        m_new = jnp.maximum(m_prev, rm)
        corr_ref[...] = jnp.exp(m_prev - m_new)
        m_ref[...] = m_new
        lsum = jnp.zeros((LANE, LANE), F32)
        pv = None
        for i in range(P):
            sub = m_prev - page_off(j - 1, i)
            for b in range(n_blk):
                p = jnp.exp(s_r[i, :, b * LANE:(b + 1) * LANE] - sub)
                lsum = lsum + p
                p_ref[i, :, b * LANE:(b + 1) * LANE] = p.astype(BF16)
            v2 = v_refs[i][...].reshape(PAGE * H_C, hd).astype(BF16)
            d = _dot(p_ref[i], v2)
            pv = d if pv is None else pv + d
        live = j > 0
        l_ref[...] = l_ref[...] * corr_prev + jnp.where(live, _row_sum(lsum), 0.0)
        acc_ref[...] = acc_ref[...] * corr_prev + jnp.where(live, pv, 0.0)

    pl.when(j % 2 == 0)(functools.partial(body, sa_ref, sb_ref))
    pl.when(j % 2 == 1)(functools.partial(body, sb_ref, sa_ref))

    @pl.when(j == n_steps)
    def _():
        s = lax.dot_general(qall, kn_ref[...], _NT, preferred_element_type=F32) + t1n_ref[...]
        off = slope_ref[...] * float(n_pages * PAGE)
        m_prev = m_ref[...]
        m_fin = jnp.maximum(m_prev, _row_max(s) + off)
        c = jnp.exp(m_prev - m_fin)
        p = jnp.exp(s - (m_fin - off))
        l = l_ref[...] * c + _row_sum(p)
        o = (acc_ref[...] * c + _dot(p.astype(BF16), vn_ref[...])) / l
        lam = _diff_lambda(lq1_ref, lk1_ref, lq2_ref, lk2_ref, lam_init)
        for h in range(H_C):
            od = o[2 * T * h:2 * T * h + T] - lam * o[2 * T * h + T:2 * T * (h + 1)]
            o_ref[:, hd * h:hd * (h + 1)] = (_head_rms(od) * gain_ref[...] * (1.0 - lam_init)).astype(BF16)


def _diff_sample(q, k_new, v_new, cache_k, cache_v, layer, page_table, lams, gain, lam_init):
    bsz, T, _ = q.shape
    n_pages = page_table.shape[1]
    hd = 2 * DH_C
    P = PAGES_PER_STEP
    n_rows = 2 * T * H_C
    assert n_rows == LANE and n_pages % P == 0
    c = np.arange(n_rows)
    c_h, c_t = c // (2 * T), c % T
    slope_c = 2.0 ** (-8.0 * (c_h + 1) / H_C)
    r = np.arange(PAGE * H_C)
    r_key, r_h = r // H_C, r % H_C
    same_head = r_h[None, :] == c_h[:, None]
    t1 = np.where(same_head, slope_c[:, None] * r_key[None, :], NEG)
    new_ok = same_head[:, :LANE] & (r_key[None, :LANE] <= c_t[:, None]) & (r_key[None, :LANE] < T)
    t1n = np.where(new_ok, slope_c[:, None] * r_key[None, :LANE], NEG)
    slope_tile = np.broadcast_to(slope_c[:, None], (n_rows, LANE))
    pad = lambda a: jnp.pad(a.reshape(bsz, T * H_C, hd), ((0, 0), (0, LANE - T * H_C), (0, 0)))

    n_steps = n_pages // P

    def page_spec(i, lag):
        def index(b, j, pt):
            step = jnp.maximum(j - 1, 0) if lag else jnp.minimum(j, n_steps - 1)
            return (layer, pt[b, step * P + i], 0, 0, 0)
        return pl.BlockSpec((None, None, PAGE, H_C, hd), index)

    const = lambda shape: pl.BlockSpec(shape, lambda b, j, pt: (0,) * len(shape))
    per_b = lambda shape: pl.BlockSpec((None,) + shape, lambda b, j, pt: (b,) + (0,) * len(shape))
    scores = (P, n_rows, PAGE * H_C)
    stat = pltpu.VMEM((n_rows, LANE), F32)
    grid_spec = pltpu.PrefetchScalarGridSpec(
        num_scalar_prefetch=1, grid=(bsz, n_steps + 1),
        in_specs=[per_b((T, D_MODEL)), per_b((LANE, hd)), per_b((LANE, hd)),
                  const((n_rows, PAGE * H_C)), const((n_rows, LANE)), const((n_rows, LANE)),
                  const((1, DH_C)), const((1, DH_C)), const((1, DH_C)), const((1, DH_C)), const((1, hd))]
                 + [page_spec(i, False) for i in range(P)] + [page_spec(i, True) for i in range(P)],
        out_specs=per_b((T, D_MODEL)),
        scratch_shapes=[pltpu.VMEM((n_rows, hd), BF16), stat, stat, stat, pltpu.VMEM((n_rows, hd), F32),
                        pltpu.VMEM(scores, F32), pltpu.VMEM(scores, F32), pltpu.VMEM(scores, BF16)])
    return pl.pallas_call(
        functools.partial(_dsample_kernel, lam_init=lam_init, n_pages=n_pages),
        out_shape=jax.ShapeDtypeStruct((bsz, T, D_MODEL), BF16), grid_spec=grid_spec,
        compiler_params=_params(("arbitrary", "arbitrary"), 48), name="diff_sample")(
            page_table, q, pad(k_new), pad(v_new), jnp.asarray(t1, F32), jnp.asarray(t1n, F32),
            jnp.asarray(slope_tile, F32), *lams, gain, *([cache_k] * P), *([cache_v] * P))


def _cross_kernel(x_ref, g_ref, wq_ref, mk_ref, mv_ref, wo_ref, o_ref):
    G, T, _ = x_ref.shape
    x = x_ref[...].reshape(G * T, D_MODEL)
    q = _dot(_rms(x, g_ref[...]).astype(BF16), wq_ref[...])
    rows = []
    for g in range(G):
        heads = []
        for h in range(H_X):
            cols = slice(DH_X * h, DH_X * (h + 1))
            qh = q[g * T:(g + 1) * T, cols].astype(BF16)
            s = lax.dot_general(qh, mk_ref[g, :, cols].astype(BF16), _NT, preferred_element_type=F32) * DH_X ** -0.5
            p = jnp.exp(s - jnp.max(s, axis=-1, keepdims=True))
            p = p / jnp.sum(p, axis=-1, keepdims=True)
            heads.append(_dot(p.astype(BF16), mv_ref[g, :, cols].astype(BF16)))
        rows.append(jnp.concatenate(heads, axis=1))
    o = jnp.concatenate(rows, axis=0).astype(BF16)
    o_ref[...] = (x + _dot(o, wo_ref[...])).reshape(G, T, D_MODEL)


def _cross(x, gain, wq, mk, mv, wo, group, rows):
    bsz, L, _ = x.shape
    xs = pl.BlockSpec((group, rows, D_MODEL), lambda b, t: (b, t, 0))
    mem = pl.BlockSpec((group, N_MEM, D_MODEL), lambda b, t: (b, 0, 0))
    return pl.pallas_call(
        _cross_kernel, out_shape=jax.ShapeDtypeStruct(x.shape, F32), grid=(bsz // group, L // rows),
        in_specs=[xs, _resident((1, D_MODEL)), _resident((D_MODEL, D_MODEL)), mem, mem,
                  _resident((D_MODEL, D_MODEL))],
        out_specs=xs, compiler_params=_params(("arbitrary", "arbitrary"), 48), name="cross")(
            x, gain.reshape(1, D_MODEL), wq, mk, mv, wo)


def _cross_cache_kernel(x_ref, g_ref, wq_ref, mk_ref, mv_ref, wo_ref, o_ref):
    G, T, _ = x_ref.shape
    n_kv = N_MEM * H_X
    x = x_ref[...].reshape(G * T, D_MODEL)
    q = _dot(_rms(x, g_ref[...]).astype(BF16), wq_ref[...])
    same = (lax.broadcasted_iota(jnp.int32, (H_X * T, n_kv), 0) // T
            == lax.broadcasted_iota(jnp.int32, (H_X * T, n_kv), 1) % H_X)
    rows = []
    for g in range(G):
        qg = q[g * T:(g + 1) * T]
        q4 = jnp.concatenate([qg[:, DH_X * h:DH_X * (h + 1)] for h in range(H_X)], axis=0).astype(BF16)
        k2 = mk_ref[g].reshape(n_kv, DH_X).astype(BF16)
        v2 = mv_ref[g].reshape(n_kv, DH_X).astype(BF16)
        s = lax.dot_general(q4, k2, _NT, preferred_element_type=F32) * DH_X ** -0.5
        s = jnp.where(same, s, NEG)
        p = jnp.exp(s - jnp.max(s, axis=-1, keepdims=True))
        p = p / jnp.sum(p, axis=-1, keepdims=True)
        o4 = _dot(p.astype(BF16), v2)
        rows.append(jnp.concatenate([o4[T * h:T * (h + 1)] for h in range(H_X)], axis=1))
    o = jnp.concatenate(rows, axis=0).astype(BF16)
    o_ref[...] = (x + _dot(o, wo_ref[...])).reshape(G, T, D_MODEL)


def _cross_cache(x, gain, wq, cache_k, cache_v, layer, wo, group):
    bsz, T, _ = x.shape
    xs = pl.BlockSpec((group, T, D_MODEL), lambda b: (b, 0, 0))
    mem = pl.BlockSpec((None, group, N_MEM, H_X, DH_X), lambda b: (layer, b, 0, 0, 0))
    return pl.pallas_call(
        _cross_cache_kernel, out_shape=jax.ShapeDtypeStruct(x.shape, F32), grid=(bsz // group,),
        in_specs=[xs, _resident((1, D_MODEL)), _resident((D_MODEL, D_MODEL)), mem, mem,
                  _resident((D_MODEL, D_MODEL))],
        out_specs=xs, compiler_params=_params(("arbitrary",), 56), name="cross_cache")(
            x, gain.reshape(1, D_MODEL), wq, cache_k, cache_v, wo)


def kernel(x_prompt, x_sample, state_ret, state_gla, cache_k, cache_v, cache_mem_k, cache_mem_v, page_table, mem_prompt, norm_ffn1, ffn1_wg, ffn1_wu, ffn1_wd, norm_mix, w_in_even, gla_wa2, gla_ba, ret_gain, gla_gain, w_in_odd, lam_q1, lam_k1, lam_q2, lam_k2, diff_gain, w_mix_out, norm_x, x_wq, x_wk, x_wv, x_wo, norm_ffn2, ffn2_wg, ffn2_wu, ffn2_wd, final_norm):
    bp, lp, _ = x_prompt.shape
    bs, ls, _ = x_sample.shape
    depth = norm_ffn1.shape[0]
    bf = lambda w: w.astype(BF16)
    xp = x_prompt.reshape(bp * lp, D_MODEL)
    xs = x_sample.reshape(bs * ls, D_MODEL)
    mem = mem_prompt.reshape(bp * N_MEM, D_MODEL)
    slopes = jnp.asarray([2.0 ** (-8.0 * (h + 1) / H_C) for h in range(H_C)], F32)
    ret_p, ret_s, gla_p, gla_s, kr_p, vr_p, kr_s, vr_s, mk_p, mv_p = ([] for _ in range(10))

    for li in range(depth):
        w1 = (bf(ffn1_wg[li]), bf(ffn1_wu[li]), bf(ffn1_wd[li]))
        xp = _ffn(xp, norm_ffn1[li], *w1)
        xs = _ffn(xs, norm_ffn1[li], *w1)
        w_out = bf(w_mix_out[li])
        if li % 2 == 0:
            e = li // 2
            w_in = bf(jnp.pad(w_in_even[e], ((0, 0), (0, Z_W - w_in_even.shape[2]))))
            wa2 = bf(jnp.pad(gla_wa2[e], ((0, LANE - GLA_RANK), (0, 0))))
            prm = (wa2, gla_ba[e].reshape(1, -1), ret_gain[e].reshape(1, -1), gla_gain[e].reshape(1, -1))
            (zp,) = _proj(xp, norm_mix[li], w_in, ((0, Z_W, F32),))
            (zs,) = _proj(xs, norm_mix[li], w_in, ((0, Z_W, F32),))
            zero = jnp.zeros((bp, H_A, DK_A, DV_A), F32)
            op, sr, sg = _even_mixer(zp.reshape(bp, lp, Z_W), zero, zero, *prm, valid=CHUNK)
            ret_p.append(sr); gla_p.append(sg)
            zs = jnp.pad(zs.reshape(bs, ls, Z_W), ((0, 0), (0, CHUNK - ls), (0, 0)))
            os_, sr, sg = _even_mixer(zs, state_ret[e], state_gla[e], *prm, valid=ls)
            ret_s.append(sr); gla_s.append(sg)
            os_ = os_[:, :ls]
        else:
            o = li // 2
            lam_init = 0.8 - 0.6 * math.exp(-0.3 * li)
            lams = tuple(a[o].reshape(1, DH_C) for a in (lam_q1, lam_k1, lam_q2, lam_k2))
            gain = diff_gain[o].reshape(1, 2 * DH_C)
            w_in = bf(w_in_odd[o])
            hd = 2 * DH_C
            outs = ((0, D_MODEL, BF16), (D_MODEL, D_MODEL, BF16), (2 * D_MODEL, D_MODEL, BF16),
                    (D_MODEL, D_MODEL, hd), (2 * D_MODEL, D_MODEL, hd))
            qp, kp, vp, kp4, vp4 = _proj(xp, norm_mix[li], w_in, outs)
            qs, ks, vs, ks4, vs4 = _proj(xs, norm_mix[li], w_in, outs)
            shp = lambda a: a.reshape(bp, lp, D_MODEL)
            shs = lambda a: a.reshape(bs, ls, D_MODEL)
            op = _diff_prompt(shp(qp), shp(kp), shp(vp), slopes, lams, gain, lam_init)
            os_ = _diff_sample(shs(qs), shs(ks), shs(vs), cache_k, cache_v, o, page_table, lams, gain, lam_init)
            kr_p.append(kp4.reshape(bp, lp, H_C, hd)); vr_p.append(vp4.reshape(bp, lp, H_C, hd))
            kr_s.append(ks4.reshape(bs, ls, H_C, hd)); vr_s.append(vs4.reshape(bs, ls, H_C, hd))
        xp = _outproj(op.reshape(bp * lp, D_MODEL), w_out, xp)
        xs = _outproj(os_.reshape(bs * ls, D_MODEL), w_out, xs)
        w_kv = bf(jnp.concatenate([x_wk[li], x_wv[li]], axis=1))
        mk, mv, mk4, mv4 = _proj(mem, None, w_kv, ((0, D_MODEL, BF16), (D_MODEL, D_MODEL, BF16),
                                                    (0, D_MODEL, DH_X), (D_MODEL, D_MODEL, DH_X)))
        mk_p.append(mk4.reshape(bp, N_MEM, H_X, DH_X)); mv_p.append(mv4.reshape(bp, N_MEM, H_X, DH_X))
        wq, wo = bf(x_wq[li]), bf(x_wo[li])
        xp = _cross(xp.reshape(bp, lp, D_MODEL), norm_x[li], wq, mk.reshape(bp, N_MEM, D_MODEL),
                    mv.reshape(bp, N_MEM, D_MODEL), wo, group=1, rows=512).reshape(bp * lp, D_MODEL)
        xs = _cross_cache(xs.reshape(bs, ls, D_MODEL), norm_x[li], wq, cache_mem_k, cache_mem_v, li, wo,
                          group=4).reshape(bs * ls, D_MODEL)
        w2 = (bf(ffn2_wg[li]), bf(ffn2_wu[li]), bf(ffn2_wd[li]))
        fg = final_norm if li == depth - 1 else None
        xp = _ffn(xp, norm_ffn2[li], *w2, final_gain=fg)
        xs = _ffn(xs, norm_ffn2[li], *w2, final_gain=fg)

    return (xp.reshape(bp, lp, D_MODEL), xs.reshape(bs, ls, D_MODEL),
            jnp.stack(ret_p), jnp.stack(ret_s), jnp.stack(gla_p), jnp.stack(gla_s),
            jnp.stack(kr_p), jnp.stack(vr_p), jnp.stack(kr_s), jnp.stack(vr_s),
            jnp.stack(mk_p), jnp.stack(mv_p))
```

```python
import functools
import math

import numpy as np
import jax
import jax.numpy as jnp
from jax import lax
from jax.experimental import pallas as pl
from jax.experimental.pallas import tpu as pltpu

F32 = jnp.float32
BF16 = jnp.bfloat16

D_MODEL = 1024
D_FF = 2816
EPS = 1e-6
CHUNK = 64
H_A, DK_A, DV_A = 4, 64, 128
GLA_RANK = 16
GLA_TAU = 16.0
H_C, DH_C = 8, 64
H_X, DH_X = 4, 256
N_MEM = 256
PAGE = 128
NEG = -1e30

LANE = 128
FF_CHUNK = 256
MIB = 1024 * 1024

Z_RQ, Z_RK, Z_RV, Z_RG, Z_GQ, Z_GK, Z_GV, Z_GG, Z_GLR = 0, 256, 512, 1024, 1536, 1792, 2048, 2560, 3072
Z_W = 3200

_NT = (((1,), (1,)), ((), ()))
_TN = (((0,), (0,)), ((), ()))


def _params(sem, vmem_mib):
    return pltpu.CompilerParams(dimension_semantics=sem, vmem_limit_bytes=vmem_mib * MIB)


def _resident(shape):
    nd = len(shape)
    return pl.BlockSpec(shape, lambda *_: (0,) * nd, pipeline_mode=pl.Buffered(1))


def _rms(x, g):
    return x * lax.rsqrt(jnp.mean(x * x, axis=-1, keepdims=True) + EPS) * g


def _head_rms(o):
    return o * lax.rsqrt(jnp.mean(o * o, axis=-1, keepdims=True) + EPS)


def _silu(x):
    return x * jax.nn.sigmoid(x)


def _dot(a, b):
    return jnp.dot(a, b, preferred_element_type=F32)


def _ffn_kernel(x_ref, g_ref, wg_ref, wu_ref, wd_ref, *rest, final):
    if final:
        fg_ref, o_ref, a_ref = rest
    else:
        o_ref, a_ref = rest
    x = x_ref[...]
    h = _rms(x, g_ref[...]).astype(BF16)
    for c in range(D_FF // FF_CHUNK):
        sl = slice(c * FF_CHUNK, (c + 1) * FF_CHUNK)
        g = _dot(h, wg_ref[:, sl])
        u = _dot(h, wu_ref[:, sl])
        a_ref[:, sl] = (_silu(g) * u).astype(BF16)
    y = x + 0.5 * _dot(a_ref[...], wd_ref[...])
    if final:
        y = _rms(y, fg_ref[...])
    o_ref[...] = y


def _ffn(x, gain, wg, wu, wd, final_gain=None):
    n = x.shape[0]
    tm = min(512, n)
    final = final_gain is not None
    row = pl.BlockSpec((tm, D_MODEL), lambda i: (i, 0))
    in_specs = [row, _resident((1, D_MODEL)), _resident((D_MODEL, D_FF)), _resident((D_MODEL, D_FF)),
                _resident((D_FF, D_MODEL))]
    args = [x, gain.reshape(1, D_MODEL), wg, wu, wd]
    if final:
        in_specs.append(_resident((1, D_MODEL)))
        args.append(final_gain.reshape(1, D_MODEL))
    return pl.pallas_call(
        functools.partial(_ffn_kernel, final=final),
        out_shape=jax.ShapeDtypeStruct((n, D_MODEL), F32),
        grid=(n // tm,), in_specs=in_specs, out_specs=row,
        scratch_shapes=[pltpu.VMEM((tm, D_FF), BF16)],
        compiler_params=_params(("arbitrary",), 48), name="ffn")(*args)


def _proj_kernel(x_ref, *rest, has_norm, outs):
    if has_norm:
        g_ref, w_ref, *o_refs = rest
        h = _rms(x_ref[...], g_ref[...]).astype(BF16)
    else:
        w_ref, *o_refs = rest
        h = x_ref[...].astype(BF16)
    done = {}
    for o_ref, (off, n, kind) in zip(o_refs, outs):
        if (off, n) not in done:
            done[(off, n)] = _dot(h, w_ref[:, off:off + n])
        y = done[(off, n)]
        if isinstance(kind, int):
            for hh in range(n // kind):
                o_ref[:, hh, :] = y[:, hh * kind:(hh + 1) * kind]
        else:
            o_ref[...] = y.astype(o_ref.dtype)


def _proj(x, gain, w, outs):
    n, k = x.shape
    tm = min(512, n)
    has_norm = gain is not None
    in_specs = [pl.BlockSpec((tm, k), lambda i: (i, 0))]
    args = [x]
    if has_norm:
        in_specs.append(_resident((1, k)))
        args.append(gain.reshape(1, k))
    in_specs.append(_resident(w.shape))
    args.append(w)
    out_shape, out_specs = [], []
    for _, width, kind in outs:
        if isinstance(kind, int):
            out_shape.append(jax.ShapeDtypeStruct((n, width // kind, kind), F32))
            out_specs.append(pl.BlockSpec((tm, width // kind, kind), lambda i: (i, 0, 0)))
        else:
            out_shape.append(jax.ShapeDtypeStruct((n, width), kind))
            out_specs.append(pl.BlockSpec((tm, width), lambda i: (i, 0)))
    return pl.pallas_call(
        functools.partial(_proj_kernel, has_norm=has_norm, outs=outs),
        out_shape=out_shape, grid=(n // tm,), in_specs=in_specs, out_specs=out_specs,
        compiler_params=_params(("arbitrary",), 40), name="proj")(*args)


def _outproj_kernel(a_ref, w_ref, r_ref, o_ref):
    o_ref[...] = r_ref[...] + _dot(a_ref[...], w_ref[...])


def _outproj(a, w, res):
    n, k = a.shape
    tm = min(512, n)
    return pl.pallas_call(
        _outproj_kernel, out_shape=jax.ShapeDtypeStruct((n, D_MODEL), F32), grid=(n // tm,),
        in_specs=[pl.BlockSpec((tm, k), lambda i: (i, 0)), _resident(w.shape),
                  pl.BlockSpec((tm, D_MODEL), lambda i: (i, 0))],
        out_specs=pl.BlockSpec((tm, D_MODEL), lambda i: (i, 0)),
        compiler_params=_params(("arbitrary",), 32), name="outproj")(a, w, res)


_LOG_GAMMA = [math.log1p(-2.0 ** (-5.0 - h)) for h in range(H_A)]


def _even_kernel(z_ref, sr0_ref, sg0_ref, wa2_ref, ba_ref, gr_ref, gg_ref, o_ref, sr_ref, sg_ref, st_ref,
                 *, n_chunks, valid):
    C = CHUNK
    step = pl.program_id(1)

    def pair_state(a, b):
        zero = jnp.zeros((DK_A, DV_A), F32)
        s = jnp.concatenate([jnp.concatenate([a, zero], axis=1), jnp.concatenate([zero, b], axis=1)], axis=0)
        return s.T

    @pl.when(step == 0)
    def _():
        for pp in range(2):
            st_ref[pp] = pair_state(sr0_ref[2 * pp], sr0_ref[2 * pp + 1])
            st_ref[2 + pp] = pair_state(sg0_ref[2 * pp], sg0_ref[2 * pp + 1])

    row = lax.broadcasted_iota(jnp.int32, (C, LANE), 0)
    lane = lax.broadcasted_iota(jnp.int32, (C, LANE), 1)
    lo = lane < DK_A
    r2 = lax.broadcasted_iota(jnp.int32, (2 * C, C), 0)
    c2 = lax.broadcasted_iota(jnp.int32, (2 * C, C), 1)
    causal2 = jnp.where(r2 >= C, r2 - C, r2) >= c2
    tr = lax.broadcasted_iota(jnp.int32, (C, C), 0)
    tc = lax.broadcasted_iota(jnp.int32, (C, C), 1)
    tril = (tr >= tc).astype(F32)
    sr_ = lax.broadcasted_iota(jnp.int32, (2 * DV_A, 2 * DK_A), 0)
    sc_ = lax.broadcasted_iota(jnp.int32, (2 * DV_A, 2 * DK_A), 1)
    blockdiag = (sr_ >= DV_A) == (sc_ >= DK_A)
    steps_done = jnp.minimum(row + 1, valid).astype(F32)

    for ci in range(n_chunks):
        rows = slice(ci * C, (ci + 1) * C)
        glr = z_ref[rows, Z_GLR:Z_GLR + LANE].astype(BF16)
        xg = _dot(glr, wa2_ref[...]) + ba_ref[...]
        la = (jnp.minimum(xg, 0.0) - jnp.log1p(jnp.exp(-jnp.abs(xg)))) / GLA_TAU
        if valid < C:
            la = jnp.where(lax.broadcasted_iota(jnp.int32, la.shape, 0) < valid, la, 0.0)
        b_gla = jnp.dot(tril, la, precision=lax.Precision.HIGHEST, preferred_element_type=F32)
        for p in range(4):
            pp = p % 2
            if p < 2:
                q = z_ref[rows, Z_RQ + LANE * pp:Z_RQ + LANE * (pp + 1)]
                k = z_ref[rows, Z_RK + LANE * pp:Z_RK + LANE * (pp + 1)] * DK_A ** -0.5
                v = z_ref[rows, Z_RV + 2 * LANE * pp:Z_RV + 2 * LANE * (pp + 1)]
                b = steps_done * jnp.where(lo, _LOG_GAMMA[2 * pp], _LOG_GAMMA[2 * pp + 1])
                z_gate, gain_ref, o_col = Z_RG, gr_ref, 0
            else:
                q = z_ref[rows, Z_GQ + LANE * pp:Z_GQ + LANE * (pp + 1)] * DK_A ** -0.5
                k = z_ref[rows, Z_GK + LANE * pp:Z_GK + LANE * (pp + 1)]
                v = z_ref[rows, Z_GV + 2 * LANE * pp:Z_GV + 2 * LANE * (pp + 1)]
                b = b_gla[:, LANE * pp:LANE * (pp + 1)]
                z_gate, gain_ref, o_col = Z_GG, gg_ref, H_A * DV_A
            b_last = b[C - 1:C, :]
            qd = q * jnp.exp(b)
            k_in = (k * jnp.exp(-b)).astype(BF16)
            k_out = (k * jnp.exp(b_last - b)).astype(BF16)
            vb = v.astype(BF16)
            q2 = jnp.concatenate([jnp.where(lo, qd, 0.0), jnp.where(lo, 0.0, qd)], axis=0).astype(BF16)
            att = lax.dot_general(q2, k_in, _NT, preferred_element_type=F32)
            att = jnp.where(causal2, att, 0.0).astype(BF16)
            st = st_ref[p]
            o = lax.dot_general(qd.astype(BF16), st.astype(BF16), _NT, preferred_element_type=F32)
            o = o + jnp.concatenate([_dot(att[:C], vb[:, :DV_A]), _dot(att[C:], vb[:, DV_A:])], axis=1)
            delta_t = lax.dot_general(vb, k_out, _TN, preferred_element_type=F32)
            st_ref[p] = st * jnp.exp(b_last) + jnp.where(blockdiag, delta_t, 0.0)
            for hh in range(2):
                head = 2 * pp + hh
                oh = _head_rms(o[:, DV_A * hh:DV_A * (hh + 1)])
                gate = z_ref[rows, z_gate + DV_A * head:z_gate + DV_A * (head + 1)]
                gain = gain_ref[:, DV_A * head:DV_A * (head + 1)]
                o_ref[rows, o_col + DV_A * head:o_col + DV_A * (head + 1)] = (oh * gain * _silu(gate)).astype(BF16)

    @pl.when(step == pl.num_programs(1) - 1)
    def _():
        for p in range(4):
            s = st_ref[p].T
            out = sr_ref if p < 2 else sg_ref
            out[2 * (p % 2)] = s[:DK_A, :DV_A]
            out[2 * (p % 2) + 1] = s[DK_A:, DV_A:]


def _even_mixer(z, s_ret, s_gla, wa2, ba, g_ret, g_gla, valid):
    bsz, L, _ = z.shape
    rows = min(L, 4 * CHUNK)
    state = pl.BlockSpec((None, H_A, DK_A, DV_A), lambda b, s: (b, 0, 0, 0))
    st_shape = jax.ShapeDtypeStruct((bsz, H_A, DK_A, DV_A), F32)
    return pl.pallas_call(
        functools.partial(_even_kernel, n_chunks=rows // CHUNK, valid=valid),
        out_shape=[jax.ShapeDtypeStruct((bsz, L, D_MODEL), BF16), st_shape, st_shape],
        grid=(bsz, L // rows),
        in_specs=[pl.BlockSpec((None, rows, Z_W), lambda b, s: (b, s, 0)), state, state,
                  _resident((LANE, 2 * LANE)), _resident((1, 2 * LANE)),
                  _resident((1, H_A * DV_A)), _resident((1, H_A * DV_A))],
        out_specs=[pl.BlockSpec((None, rows, D_MODEL), lambda b, s: (b, s, 0)), state, state],
        scratch_shapes=[pltpu.VMEM((4, 2 * DV_A, 2 * DK_A), F32)],
        compiler_params=_params(("arbitrary", "arbitrary"), 32), name="even_mixer")(
            z, s_ret, s_gla, wa2, ba, g_ret, g_gla)


def _diff_lambda(lq1_ref, lk1_ref, lq2_ref, lk2_ref, lam_init):
    a = jnp.exp(jnp.sum(lq1_ref[...] * lk1_ref[...], axis=-1, keepdims=True))
    b = jnp.exp(jnp.sum(lq2_ref[...] * lk2_ref[...], axis=-1, keepdims=True))
    return a - b + lam_init


def _two_maps(q):
    lane = lax.broadcasted_iota(jnp.int32, q.shape, 1)
    lo = lane < DH_C
    return jnp.concatenate([jnp.where(lo, q, 0.0), jnp.where(lo, 0.0, q)], axis=0)


def _dprompt_kernel(slope_ref, qa_ref, qb_ref, k_ref, v_ref, kpos_ref, lq1_ref, lk1_ref, lq2_ref, lk2_ref, gain_ref,
                    oa_ref, ob_ref, kaug_ref, vt_ref, s_ref, *, tq, lam_init):
    h = pl.program_id(1)
    i = pl.program_id(2)
    nk = vt_ref.shape[0]

    @pl.when(i == 0)
    def _():
        kaug_ref[:, :LANE] = k_ref[...]
        kaug_ref[:, LANE:] = kpos_ref[...]
        for j in range(nk):
            vt_ref[j] = v_ref[j * tq:(j + 1) * tq, :].astype(F32).T.astype(BF16)

    key = lax.broadcasted_iota(jnp.int32, (tq, 2 * tq), 0)
    qry = lax.broadcasted_iota(jnp.int32, (tq, 2 * tq), 1)
    causal = key <= jnp.where(qry >= tq, qry - tq, qry)
    lam = _diff_lambda(lq1_ref, lk1_ref, lq2_ref, lk2_ref, lam_init)

    def attend(q_ref, o_ref, n, slot):
        q2 = _two_maps(q_ref[...].astype(F32) * DH_C ** -0.5)
        lane = lax.broadcasted_iota(jnp.int32, q2.shape, 1)
        q2aug = jnp.concatenate([q2, jnp.where(lane < 2, slope_ref[h], 0.0)], axis=1).astype(BF16)
        m = None
        for j in range(n):
            s = lax.dot_general(kaug_ref[j * tq:(j + 1) * tq, :], q2aug, _NT, preferred_element_type=F32)
            if j == n - 1:
                s = jnp.where(causal, s, NEG)
            s_ref[slot + j] = s
            mj = jnp.max(s, axis=0, keepdims=True)
            m = mj if m is None else jnp.maximum(m, mj)
        l = None
        acc = None
        for j in range(n):
            p = jnp.exp(s_ref[slot + j] - m)
            lj = jnp.sum(p, axis=0, keepdims=True)
            aj = _dot(vt_ref[j], p.astype(BF16))
            l = lj if l is None else l + lj
            acc = aj if acc is None else acc + aj
        o = acc / l
        od = (o[:, :tq] - lam * o[:, tq:]).T
        o_ref[...] = (_head_rms(od) * gain_ref[...] * (1.0 - lam_init)).astype(BF16)

    def pair(v):
        attend(qa_ref, oa_ref, v + 1, 0)
        attend(qb_ref, ob_ref, nk - v, v + 1)

    for v in range(nk // 2):
        pl.when(i == v)(functools.partial(pair, v))


def _diff_prompt(q, k, v, slopes, lams, gain, lam_init):
    bsz, L, _ = q.shape
    tq = 256
    nq = L // tq
    hd = 2 * DH_C
    pos = np.arange(L)
    kpos = np.zeros((L, LANE), np.float32)
    kpos[:, 0] = pos // 16 * 16
    kpos[:, 1] = pos % 16
    assert L <= 4096 and L % (2 * tq) == 0
    kv = pl.BlockSpec((None, L, hd), lambda b, h, i: (b, 0, h))
    vec = lambda n: pl.BlockSpec((1, n), lambda b, h, i: (0, 0))
    half = jax.ShapeDtypeStruct((bsz, L // 2, D_MODEL), BF16)
    lo, hi = pl.pallas_call(
        functools.partial(_dprompt_kernel, tq=tq, lam_init=lam_init),
        out_shape=[half, half],
        grid=(bsz, H_C, nq // 2),
        in_specs=[pl.BlockSpec(memory_space=pltpu.SMEM),
                  pl.BlockSpec((None, tq, hd), lambda b, h, i: (b, i, h)),
                  pl.BlockSpec((None, tq, hd), lambda b, h, i: (b, nq - 1 - i, h)), kv, kv,
                  pl.BlockSpec((L, LANE), lambda b, h, i: (0, 0)),
                  vec(DH_C), vec(DH_C), vec(DH_C), vec(DH_C), vec(hd)],
        out_specs=[pl.BlockSpec((None, tq, hd), lambda b, h, i: (b, i, h)),
                   pl.BlockSpec((None, tq, hd), lambda b, h, i: (b, nq // 2 - 1 - i, h))],
        scratch_shapes=[pltpu.VMEM((L, 2 * LANE), BF16), pltpu.VMEM((nq, hd, tq), BF16),
                        pltpu.VMEM((nq + 1, tq, 2 * tq), F32)],
        compiler_params=_params(("arbitrary", "arbitrary", "arbitrary"), 32), name="diff_prompt")(
            slopes, q, q, k, v, jnp.asarray(kpos, BF16), *lams, gain)
    return jnp.concatenate([lo, hi], axis=1)


PAGES_PER_STEP = 8
RING = 3


def _row_max(s):
    m = s[:, :LANE]
    for j in range(1, s.shape[1] // LANE):
        m = jnp.maximum(m, s[:, j * LANE:(j + 1) * LANE])
    return jnp.broadcast_to(jnp.max(m, axis=1, keepdims=True), m.shape)


def _row_sum(p):
    return jnp.broadcast_to(jnp.sum(p, axis=1, keepdims=True), p.shape)


def _dsample_kernel(pt_ref, q_ref, kn_ref, vn_ref, t1_ref, t1n_ref, slope_ref, lq1_ref, lk1_ref, lq2_ref,
                    lk2_ref, gain_ref, ck_hbm, cv_hbm, o_ref, qall_ref, m_ref, corr_ref, l_ref, acc_ref, sa_ref,
                    sb_ref, p_ref, kbuf, vbuf, ksem, vsem, *, lam_init, n_pages, layer):
    P = PAGES_PER_STEP
    b = pl.program_id(0)
    j = pl.program_id(1)
    n_batch = pl.num_programs(0)
    n_steps = n_pages // P
    T = q_ref.shape[0]
    hd = 2 * DH_C
    n_blk = PAGE * H_C // LANE
    wide = 2 * LANE

    def page_copies(hbm, buf, sem, row, step):
        slot = (row * n_steps + step) % RING
        return [pltpu.make_async_copy(hbm.at[layer, pt_ref[row, step * P + i]], buf.at[slot, i], sem.at[slot])
                for i in range(P)]

    def start_for(row, jj):
        @pl.when(jnp.logical_and(row < n_batch, jj < n_steps))
        def _():
            for c in page_copies(ck_hbm, kbuf, ksem, row, jj):
                c.start()

        @pl.when(jnp.logical_and(row < n_batch, jj >= 1))
        def _():
            for c in page_copies(cv_hbm, vbuf, vsem, row, jj - 1):
                c.start()

    @pl.when(jnp.logical_and(b == 0, j == 0))
    def _():
        start_for(b, j)
        start_for(b, j + 1)

    ahead = b * (n_steps + 1) + j + 2
    start_for(ahead // (n_steps + 1), ahead % (n_steps + 1))

    @pl.when(j == 0)
    def _():
        for h in range(H_C):
            qh = q_ref[:, hd * h:hd * (h + 1)].astype(F32) * DH_C ** -0.5
            qall_ref[2 * T * h:2 * T * (h + 1), :] = _two_maps(qh).astype(BF16)
        m_ref[...] = jnp.full_like(m_ref, NEG)
        corr_ref[...] = jnp.ones_like(corr_ref)
        l_ref[...] = jnp.zeros_like(l_ref)
        acc_ref[...] = jnp.zeros_like(acc_ref)

    qall = qall_ref[...]
    page_off = lambda step, i: slope_ref[...] * ((step * P + i) * PAGE).astype(F32)

    def score(s_w, m_prev):
        for c in page_copies(ck_hbm, kbuf, ksem, b, j):
            c.wait()
        slot = (b * n_steps + j) % RING
        rm = None
        for i in range(P):
            pm = None
            for blk in range(PAGE * H_C // wide):
                keys = slice(blk * wide // H_C, (blk + 1) * wide // H_C)
                k2 = kbuf[slot, i, keys].reshape(wide, hd).astype(BF16)
                s = lax.dot_general(qall, k2, _NT, preferred_element_type=F32) + t1_ref[:, blk * wide:(blk + 1) * wide]
                s_w[i, :, blk * wide:(blk + 1) * wide] = s
                mb = jnp.maximum(s[:, :LANE], s[:, LANE:])
                pm = mb if pm is None else jnp.maximum(pm, mb)
            r = _row_max(pm) + page_off(j, i)
            rm = r if rm is None else jnp.maximum(rm, r)
        m_new = jnp.maximum(m_prev, rm)
        corr_ref[...] = jnp.exp(m_prev - m_new)
        m_ref[...] = m_new

    def apply(s_r, m_prev, corr_prev):
        for c in page_copies(cv_hbm, vbuf, vsem, b, j - 1):
            c.wait()
        slot = (b * n_steps + j - 1) % RING
        lsum = jnp.zeros((LANE, LANE), F32)
        pv = None
        for i in range(P):
            sub = m_prev - page_off(j - 1, i)
            for blk in range(n_blk):
                p = jnp.exp(s_r[i, :, blk * LANE:(blk + 1) * LANE] - sub)
                lsum = lsum + p
                p_ref[i, :, blk * LANE:(blk + 1) * LANE] = p.astype(BF16)
            v2 = vbuf[slot, i].reshape(PAGE * H_C, hd).astype(BF16)
            d = _dot(p_ref[i], v2)
            pv = d if pv is None else pv + d
        l_ref[...] = l_ref[...] * corr_prev + _row_sum(lsum)
        acc_ref[...] = acc_ref[...] * corr_prev + pv

    def both(s_w, s_r):
        m_prev = m_ref[...]
        corr_prev = corr_ref[...]
        score(s_w, m_prev)
        apply(s_r, m_prev, corr_prev)

    middle = jnp.logical_and(j > 0, j < n_steps)
    pl.when(j == 0)(lambda: score(sa_ref, m_ref[...]))
    pl.when(jnp.logical_and(middle, j % 2 == 0))(functools.partial(both, sa_ref, sb_ref))
    pl.when(jnp.logical_and(middle, j % 2 == 1))(functools.partial(both, sb_ref, sa_ref))
    last_scores = sb_ref if n_steps % 2 == 0 else sa_ref
    pl.when(j == n_steps)(lambda: apply(last_scores, m_ref[...], corr_ref[...]))

    @pl.when(j == n_steps)
    def _():
        s = lax.dot_general(qall, kn_ref[...], _NT, preferred_element_type=F32) + t1n_ref[...]
        off = slope_ref[...] * float(n_pages * PAGE)
        m_prev = m_ref[...]
        m_fin = jnp.maximum(m_prev, _row_max(s) + off)
        c = jnp.exp(m_prev - m_fin)
        p = jnp.exp(s - (m_fin - off))
        l = l_ref[...] * c + _row_sum(p)
        o = (acc_ref[...] * c + _dot(p.astype(BF16), vn_ref[...])) / l
        lam = _diff_lambda(lq1_ref, lk1_ref, lq2_ref, lk2_ref, lam_init)
        for h in range(H_C):
            od = o[2 * T * h:2 * T * h + T] - lam * o[2 * T * h + T:2 * T * (h + 1)]
            o_ref[:, hd * h:hd * (h + 1)] = (_head_rms(od) * gain_ref[...] * (1.0 - lam_init)).astype(BF16)


def _diff_sample(q, k_new, v_new, cache_k, cache_v, layer, page_table, lams, gain, lam_init):
    bsz, T, _ = q.shape
    n_pages = page_table.shape[1]
    hd = 2 * DH_C
    P = PAGES_PER_STEP
    n_rows = 2 * T * H_C
    assert n_rows == LANE and n_pages % P == 0
    c = np.arange(n_rows)
    c_h, c_t = c // (2 * T), c % T
    slope_c = 2.0 ** (-8.0 * (c_h + 1) / H_C)
    r = np.arange(PAGE * H_C)
    r_key, r_h = r // H_C, r % H_C
    same_head = r_h[None, :] == c_h[:, None]
    t1 = np.where(same_head, slope_c[:, None] * r_key[None, :], NEG)
    new_ok = same_head[:, :LANE] & (r_key[None, :LANE] <= c_t[:, None]) & (r_key[None, :LANE] < T)
    t1n = np.where(new_ok, slope_c[:, None] * r_key[None, :LANE], NEG)
    slope_tile = np.broadcast_to(slope_c[:, None], (n_rows, LANE))
    pad = lambda a: jnp.pad(a.reshape(bsz, T * H_C, hd), ((0, 0), (0, LANE - T * H_C), (0, 0)))

    n_steps = n_pages // P
    const = lambda shape: pl.BlockSpec(shape, lambda b, j, pt: (0,) * len(shape))
    per_b = lambda shape: pl.BlockSpec((None,) + shape, lambda b, j, pt: (b,) + (0,) * len(shape))
    scores = (P, n_rows, PAGE * H_C)
    stat = pltpu.VMEM((n_rows, LANE), F32)
    grid_spec = pltpu.PrefetchScalarGridSpec(
        num_scalar_prefetch=1, grid=(bsz, n_steps + 1),
        in_specs=[per_b((T, D_MODEL)), per_b((LANE, hd)), per_b((LANE, hd)),
                  const((n_rows, PAGE * H_C)), const((n_rows, LANE)), const((n_rows, LANE)),
                  const((1, DH_C)), const((1, DH_C)), const((1, DH_C)), const((1, DH_C)), const((1, hd))]
                 + [pl.BlockSpec(memory_space=pl.ANY)] * 2,
        out_specs=per_b((T, D_MODEL)),
        scratch_shapes=[pltpu.VMEM((n_rows, hd), BF16), stat, stat, stat, pltpu.VMEM((n_rows, hd), F32),
                        pltpu.VMEM(scores, F32), pltpu.VMEM(scores, F32), pltpu.VMEM(scores, BF16),
                        pltpu.VMEM((RING, P, PAGE, H_C, hd), F32), pltpu.VMEM((RING, P, PAGE, H_C, hd), F32),
                        pltpu.SemaphoreType.DMA((RING,)), pltpu.SemaphoreType.DMA((RING,))])
    return pl.pallas_call(
        functools.partial(_dsample_kernel, lam_init=lam_init, n_pages=n_pages, layer=layer),
        out_shape=jax.ShapeDtypeStruct((bsz, T, D_MODEL), BF16), grid_spec=grid_spec,
        compiler_params=_params(("arbitrary", "arbitrary"), 48), name="diff_sample")(
            page_table, q, pad(k_new), pad(v_new), jnp.asarray(t1, F32), jnp.asarray(t1n, F32),
            jnp.asarray(slope_tile, F32), *lams, gain, cache_k, cache_v)


def _cross_kernel(x_ref, g_ref, wq_ref, mk_ref, mv_ref, wo_ref, o_ref):
    G, T, _ = x_ref.shape
    x = x_ref[...].reshape(G * T, D_MODEL)
    q = _dot(_rms(x, g_ref[...]).astype(BF16), wq_ref[...])
    rows = []
    for g in range(G):
        heads = []
        for h in range(H_X):
            cols = slice(DH_X * h, DH_X * (h + 1))
            qh = q[g * T:(g + 1) * T, cols].astype(BF16)
            s = lax.dot_general(qh, mk_ref[g, :, cols].astype(BF16), _NT, preferred_element_type=F32) * DH_X ** -0.5
            p = jnp.exp(s - jnp.max(s, axis=-1, keepdims=True))
            p = p / jnp.sum(p, axis=-1, keepdims=True)
            heads.append(_dot(p.astype(BF16), mv_ref[g, :, cols].astype(BF16)))
        rows.append(jnp.concatenate(heads, axis=1))
    o = jnp.concatenate(rows, axis=0).astype(BF16)
    o_ref[...] = (x + _dot(o, wo_ref[...])).reshape(G, T, D_MODEL)


def _cross(x, gain, wq, mk, mv, wo, group, rows):
    bsz, L, _ = x.shape
    xs = pl.BlockSpec((group, rows, D_MODEL), lambda b, t: (b, t, 0))
    mem = pl.BlockSpec((group, N_MEM, D_MODEL), lambda b, t: (b, 0, 0))
    return pl.pallas_call(
        _cross_kernel, out_shape=jax.ShapeDtypeStruct(x.shape, F32), grid=(bsz // group, L // rows),
        in_specs=[xs, _resident((1, D_MODEL)), _resident((D_MODEL, D_MODEL)), mem, mem,
                  _resident((D_MODEL, D_MODEL))],
        out_specs=xs, compiler_params=_params(("arbitrary", "arbitrary"), 48), name="cross")(
            x, gain.reshape(1, D_MODEL), wq, mk, mv, wo)


def _cross_cache_kernel(x_ref, g_ref, wq_ref, mk_ref, mv_ref, wo_ref, o_ref):
    G, T, _ = x_ref.shape
    n_kv = N_MEM * H_X
    x = x_ref[...].reshape(G * T, D_MODEL)
    q = _dot(_rms(x, g_ref[...]).astype(BF16), wq_ref[...])
    same = (lax.broadcasted_iota(jnp.int32, (H_X * T, n_kv), 0) // T
            == lax.broadcasted_iota(jnp.int32, (H_X * T, n_kv), 1) % H_X)
    rows = []
    for g in range(G):
        qg = q[g * T:(g + 1) * T]
        q4 = jnp.concatenate([qg[:, DH_X * h:DH_X * (h + 1)] for h in range(H_X)], axis=0).astype(BF16)
        k2 = mk_ref[g].reshape(n_kv, DH_X).astype(BF16)
        v2 = mv_ref[g].reshape(n_kv, DH_X).astype(BF16)
        s = lax.dot_general(q4, k2, _NT, preferred_element_type=F32) * DH_X ** -0.5
        s = jnp.where(same, s, NEG)
        p = jnp.exp(s - jnp.max(s, axis=-1, keepdims=True))
        p = p / jnp.sum(p, axis=-1, keepdims=True)
        o4 = _dot(p.astype(BF16), v2)
        rows.append(jnp.concatenate([o4[T * h:T * (h + 1)] for h in range(H_X)], axis=1))
    o = jnp.concatenate(rows, axis=0).astype(BF16)
    o_ref[...] = (x + _dot(o, wo_ref[...])).reshape(G, T, D_MODEL)


def _cross_cache(x, gain, wq, cache_k, cache_v, layer, wo, group):
    bsz, T, _ = x.shape
    xs = pl.BlockSpec((group, T, D_MODEL), lambda b: (b, 0, 0))
    mem = pl.BlockSpec((None, group, N_MEM, H_X, DH_X), lambda b: (layer, b, 0, 0, 0))
    return pl.pallas_call(
        _cross_cache_kernel, out_shape=jax.ShapeDtypeStruct(x.shape, F32), grid=(bsz // group,),
        in_specs=[xs, _resident((1, D_MODEL)), _resident((D_MODEL, D_MODEL)), mem, mem,
                  _resident((D_MODEL, D_MODEL))],
        out_specs=xs, compiler_params=_params(("arbitrary",), 56), name="cross_cache")(
            x, gain.reshape(1, D_MODEL), wq, cache_k, cache_v, wo)


def kernel(x_prompt, x_sample, state_ret, state_gla, cache_k, cache_v, cache_mem_k, cache_mem_v, page_table, mem_prompt, norm_ffn1, ffn1_wg, ffn1_wu, ffn1_wd, norm_mix, w_in_even, gla_wa2, gla_ba, ret_gain, gla_gain, w_in_odd, lam_q1, lam_k1, lam_q2, lam_k2, diff_gain, w_mix_out, norm_x, x_wq, x_wk, x_wv, x_wo, norm_ffn2, ffn2_wg, ffn2_wu, ffn2_wd, final_norm):
    bp, lp, _ = x_prompt.shape
    bs, ls, _ = x_sample.shape
    depth = norm_ffn1.shape[0]
    bf = lambda w: w.astype(BF16)
    xp = x_prompt.reshape(bp * lp, D_MODEL)
    xs = x_sample.reshape(bs * ls, D_MODEL)
    mem = mem_prompt.reshape(bp * N_MEM, D_MODEL)
    slopes = jnp.asarray([2.0 ** (-8.0 * (h + 1) / H_C) for h in range(H_C)], F32)
    ret_p, ret_s, gla_p, gla_s, kr_p, vr_p, kr_s, vr_s, mk_p, mv_p = ([] for _ in range(10))

    for li in range(depth):
        w1 = (bf(ffn1_wg[li]), bf(ffn1_wu[li]), bf(ffn1_wd[li]))
        xp = _ffn(xp, norm_ffn1[li], *w1)
        xs = _ffn(xs, norm_ffn1[li], *w1)
        w_out = bf(w_mix_out[li])
        if li % 2 == 0:
            e = li // 2
            w_in = bf(jnp.pad(w_in_even[e], ((0, 0), (0, Z_W - w_in_even.shape[2]))))
            wa2 = bf(jnp.pad(gla_wa2[e], ((0, LANE - GLA_RANK), (0, 0))))
            prm = (wa2, gla_ba[e].reshape(1, -1), ret_gain[e].reshape(1, -1), gla_gain[e].reshape(1, -1))
            (zp,) = _proj(xp, norm_mix[li], w_in, ((0, Z_W, F32),))
            (zs,) = _proj(xs, norm_mix[li], w_in, ((0, Z_W, F32),))
            zero = jnp.zeros((bp, H_A, DK_A, DV_A), F32)
            op, sr, sg = _even_mixer(zp.reshape(bp, lp, Z_W), zero, zero, *prm, valid=CHUNK)
            ret_p.append(sr); gla_p.append(sg)
            zs = jnp.pad(zs.reshape(bs, ls, Z_W), ((0, 0), (0, CHUNK - ls), (0, 0)))
            os_, sr, sg = _even_mixer(zs, state_ret[e], state_gla[e], *prm, valid=ls)
            ret_s.append(sr); gla_s.append(sg)
            os_ = os_[:, :ls]
        else:
            o = li // 2
            lam_init = 0.8 - 0.6 * math.exp(-0.3 * li)
            lams = tuple(a[o].reshape(1, DH_C) for a in (lam_q1, lam_k1, lam_q2, lam_k2))
            gain = diff_gain[o].reshape(1, 2 * DH_C)
            w_in = bf(w_in_odd[o])
            hd = 2 * DH_C
            outs = ((0, D_MODEL, BF16), (D_MODEL, D_MODEL, BF16), (2 * D_MODEL, D_MODEL, BF16),
                    (D_MODEL, D_MODEL, hd), (2 * D_MODEL, D_MODEL, hd))
            qp, kp, vp, kp4, vp4 = _proj(xp, norm_mix[li], w_in, outs)
            qs, ks, vs, ks4, vs4 = _proj(xs, norm_mix[li], w_in, outs)
            shp = lambda a: a.reshape(bp, lp, D_MODEL)
            shs = lambda a: a.reshape(bs, ls, D_MODEL)
            op = _diff_prompt(shp(qp), shp(kp), shp(vp), slopes, lams, gain, lam_init)
            os_ = _diff_sample(shs(qs), shs(ks), shs(vs), cache_k, cache_v, o, page_table, lams, gain, lam_init)
            kr_p.append(kp4.reshape(bp, lp, H_C, hd)); vr_p.append(vp4.reshape(bp, lp, H_C, hd))
            kr_s.append(ks4.reshape(bs, ls, H_C, hd)); vr_s.append(vs4.reshape(bs, ls, H_C, hd))
        xp = _outproj(op.reshape(bp * lp, D_MODEL), w_out, xp)
        xs = _outproj(os_.reshape(bs * ls, D_MODEL), w_out, xs)
        w_kv = bf(jnp.concatenate([x_wk[li], x_wv[li]], axis=1))
        mk, mv, mk4, mv4 = _proj(mem, None, w_kv, ((0, D_MODEL, BF16), (D_MODEL, D_MODEL, BF16),
                                                    (0, D_MODEL, DH_X), (D_MODEL, D_MODEL, DH_X)))
        mk_p.append(mk4.reshape(bp, N_MEM, H_X, DH_X)); mv_p.append(mv4.reshape(bp, N_MEM, H_X, DH_X))
        wq, wo = bf(x_wq[li]), bf(x_wo[li])
        xp = _cross(xp.reshape(bp, lp, D_MODEL), norm_x[li], wq, mk.reshape(bp, N_MEM, D_MODEL),
                    mv.reshape(bp, N_MEM, D_MODEL), wo, group=1, rows=512).reshape(bp * lp, D_MODEL)
        xs = _cross_cache(xs.reshape(bs, ls, D_MODEL), norm_x[li], wq, cache_mem_k, cache_mem_v, li, wo,
                          group=4).reshape(bs * ls, D_MODEL)
        w2 = (bf(ffn2_wg[li]), bf(ffn2_wu[li]), bf(ffn2_wd[li]))
        fg = final_norm if li == depth - 1 else None
        xp = _ffn(xp, norm_ffn2[li], *w2, final_gain=fg)
        xs = _ffn(xs, norm_ffn2[li], *w2, final_gain=fg)

    return (xp.reshape(bp, lp, D_MODEL), xs.reshape(bs, ls, D_MODEL),
            jnp.stack(ret_p), jnp.stack(ret_s), jnp.stack(gla_p), jnp.stack(gla_s),
            jnp.stack(kr_p), jnp.stack(vr_p), jnp.stack(kr_s), jnp.stack(vr_s),
            jnp.stack(mk_p), jnp.stack(mv_p))
```

```python
import functools
import math

import numpy as np
import jax
import jax.numpy as jnp
from jax import lax
from jax.experimental import pallas as pl
from jax.experimental.pallas import tpu as pltpu

F32 = jnp.float32
BF16 = jnp.bfloat16

D_MODEL = 1024
D_FF = 2816
EPS = 1e-6
CHUNK = 64
H_A, DK_A, DV_A = 4, 64, 128
GLA_RANK = 16
GLA_TAU = 16.0
H_C, DH_C = 8, 64
H_X, DH_X = 4, 256
N_MEM = 256
PAGE = 128
NEG = -1e30

LANE = 128
FF_CHUNK = 256
MIB = 1024 * 1024

Z_RQ, Z_RK, Z_RV, Z_RG, Z_GQ, Z_GK, Z_GV, Z_GG, Z_GLR = 0, 256, 512, 1024, 1536, 1792, 2048, 2560, 3072
Z_W = 3200

_NT = (((1,), (1,)), ((), ()))
_TN = (((0,), (0,)), ((), ()))


def _params(sem, vmem_mib):
    return pltpu.CompilerParams(dimension_semantics=sem, vmem_limit_bytes=vmem_mib * MIB)


def _resident(shape):
    nd = len(shape)
    return pl.BlockSpec(shape, lambda *_: (0,) * nd, pipeline_mode=pl.Buffered(1))


def _layer(stacked, li):
    shape = stacked.shape[1:]
    return pl.BlockSpec((None,) + shape, lambda *_: (li,) + (0,) * len(shape), pipeline_mode=pl.Buffered(1))


def _rms(x, g):
    return x * lax.rsqrt(jnp.mean(x * x, axis=-1, keepdims=True) + EPS) * g


def _head_rms(o):
    return o * lax.rsqrt(jnp.mean(o * o, axis=-1, keepdims=True) + EPS)


def _silu(x):
    return x * jax.nn.sigmoid(x)


def _dot(a, b):
    return jnp.dot(a, b, preferred_element_type=F32)


def _ffn_kernel(x_ref, g_ref, wg_ref, wu_ref, wd_ref, *rest, final):
    if final:
        fg_ref, o_ref, a_ref = rest
    else:
        o_ref, a_ref = rest
    x = x_ref[...]
    h = _rms(x, g_ref[...]).astype(BF16)
    for c in range(D_FF // FF_CHUNK):
        sl = slice(c * FF_CHUNK, (c + 1) * FF_CHUNK)
        g = _dot(h, wg_ref[:, sl])
        u = _dot(h, wu_ref[:, sl])
        a_ref[:, sl] = (_silu(g) * u).astype(BF16)
    y = x + 0.5 * _dot(a_ref[...], wd_ref[...])
    if final:
        y = _rms(y, fg_ref[...])
    o_ref[...] = y


def _ffn(x, gains, wg, wu, wd, li, final_gain=None):
    n = x.shape[0]
    tm = min(512, n)
    final = final_gain is not None
    row = pl.BlockSpec((tm, D_MODEL), lambda i: (i, 0))
    in_specs = [row] + [_layer(a, li) for a in (gains, wg, wu, wd)]
    args = [x, gains, wg, wu, wd]
    if final:
        in_specs.append(_resident((1, D_MODEL)))
        args.append(final_gain.reshape(1, D_MODEL))
    return pl.pallas_call(
        functools.partial(_ffn_kernel, final=final),
        out_shape=jax.ShapeDtypeStruct((n, D_MODEL), F32),
        grid=(n // tm,), in_specs=in_specs, out_specs=row,
        scratch_shapes=[pltpu.VMEM((tm, D_FF), BF16)],
        compiler_params=_params(("arbitrary",), 48), name="ffn")(*args)


def _heads_store(o_ref, y, width):
    for hh in range(y.shape[1] // width):
        o_ref[:, hh, :] = y[:, hh * width:(hh + 1) * width]


def _proj_kernel(x_ref, g_ref, w_ref, *o_refs, outs):
    h = _rms(x_ref[...], g_ref[...]).astype(BF16)
    done = {}
    for o_ref, (off, n, kind) in zip(o_refs, outs):
        if (off, n) not in done:
            done[(off, n)] = _dot(h, w_ref[:, off:off + n])
        y = done[(off, n)]
        if isinstance(kind, int):
            _heads_store(o_ref, y, kind)
        else:
            o_ref[...] = y.astype(o_ref.dtype)


def _proj(x, gains, li, w, wi, outs):
    n, k = x.shape
    tm = min(512, n)
    in_specs = [pl.BlockSpec((tm, k), lambda i: (i, 0)), _layer(gains, li), _layer(w, wi)]
    args = [x, gains, w]
    out_shape, out_specs = [], []
    for _, width, kind in outs:
        if isinstance(kind, int):
            out_shape.append(jax.ShapeDtypeStruct((n, width // kind, kind), F32))
            out_specs.append(pl.BlockSpec((tm, width // kind, kind), lambda i: (i, 0, 0)))
        else:
            out_shape.append(jax.ShapeDtypeStruct((n, width), kind))
            out_specs.append(pl.BlockSpec((tm, width), lambda i: (i, 0)))
    return pl.pallas_call(
        functools.partial(_proj_kernel, outs=outs),
        out_shape=out_shape, grid=(n // tm,), in_specs=in_specs, out_specs=out_specs,
        compiler_params=_params(("arbitrary",), 40), name="proj")(*args)


def _mem_kernel(x_ref, w_ref, mk_ref, mv_ref, mk4_ref, mv4_ref):
    h = x_ref[...].astype(BF16)
    for l in range(mk_ref.shape[0]):
        for t, (o2, o4) in enumerate(((mk_ref, mk4_ref), (mv_ref, mv4_ref))):
            y = _dot(h, w_ref[:, (2 * l + t) * D_MODEL:(2 * l + t + 1) * D_MODEL])
            o2[l] = y.astype(BF16)
            _heads_store(o4.at[l], y, DH_X)


def _mem_proj(mem, w, depth):
    n = mem.shape[0]
    tm = min(512, n)
    flat = pl.BlockSpec((depth, tm, D_MODEL), lambda i: (0, i, 0))
    heads = pl.BlockSpec((depth, tm, H_X, DH_X), lambda i: (0, i, 0, 0))
    return pl.pallas_call(
        _mem_kernel,
        out_shape=[jax.ShapeDtypeStruct((depth, n, D_MODEL), BF16)] * 2
        + [jax.ShapeDtypeStruct((depth, n, H_X, DH_X), F32)] * 2,
        grid=(n // tm,), in_specs=[pl.BlockSpec((tm, D_MODEL), lambda i: (i, 0)), _resident(w.shape)],
        out_specs=[flat, flat, heads, heads],
        compiler_params=_params(("arbitrary",), 48), name="mem_proj")(mem, w)


def _outproj_kernel(a_ref, w_ref, r_ref, o_ref):
    o_ref[...] = r_ref[...] + _dot(a_ref[...], w_ref[...])


def _outproj(a, w, li, res):
    n, k = a.shape
    tm = min(512, n)
    return pl.pallas_call(
        _outproj_kernel, out_shape=jax.ShapeDtypeStruct((n, D_MODEL), F32), grid=(n // tm,),
        in_specs=[pl.BlockSpec((tm, k), lambda i: (i, 0)), _layer(w, li),
                  pl.BlockSpec((tm, D_MODEL), lambda i: (i, 0))],
        out_specs=pl.BlockSpec((tm, D_MODEL), lambda i: (i, 0)),
        compiler_params=_params(("arbitrary",), 32), name="outproj")(a, w, res)


_LOG_GAMMA = [math.log1p(-2.0 ** (-5.0 - h)) for h in range(H_A)]


def _even_kernel(z_ref, sr0_ref, sg0_ref, wa2_ref, ba_ref, gr_ref, gg_ref, o_ref, sr_ref, sg_ref, st_ref,
                 *, n_chunks, valid):
    C = CHUNK
    step = pl.program_id(1)

    def pair_state(a, b):
        zero = jnp.zeros((DK_A, DV_A), F32)
        s = jnp.concatenate([jnp.concatenate([a, zero], axis=1), jnp.concatenate([zero, b], axis=1)], axis=0)
        return s.T

    @pl.when(step == 0)
    def _():
        for pp in range(2):
            st_ref[pp] = pair_state(sr0_ref[2 * pp], sr0_ref[2 * pp + 1])
            st_ref[2 + pp] = pair_state(sg0_ref[2 * pp], sg0_ref[2 * pp + 1])

    row = lax.broadcasted_iota(jnp.int32, (C, LANE), 0)
    lane = lax.broadcasted_iota(jnp.int32, (C, LANE), 1)
    lo = lane < DK_A
    r2 = lax.broadcasted_iota(jnp.int32, (2 * C, C), 0)
    c2 = lax.broadcasted_iota(jnp.int32, (2 * C, C), 1)
    causal2 = jnp.where(r2 >= C, r2 - C, r2) >= c2
    tr = lax.broadcasted_iota(jnp.int32, (C, C), 0)
    tc = lax.broadcasted_iota(jnp.int32, (C, C), 1)
    tril = (tr >= tc).astype(F32)
    sr_ = lax.broadcasted_iota(jnp.int32, (2 * DV_A, 2 * DK_A), 0)
    sc_ = lax.broadcasted_iota(jnp.int32, (2 * DV_A, 2 * DK_A), 1)
    blockdiag = (sr_ >= DV_A) == (sc_ >= DK_A)
    steps_done = jnp.minimum(row + 1, valid).astype(F32)

    for ci in range(n_chunks):
        rows = slice(ci * C, (ci + 1) * C)
        glr = z_ref[rows, Z_GLR:Z_GLR + LANE].astype(BF16)
        xg = _dot(glr, wa2_ref[...]) + ba_ref[...]
        la = (jnp.minimum(xg, 0.0) - jnp.log1p(jnp.exp(-jnp.abs(xg)))) / GLA_TAU
        if valid < C:
            la = jnp.where(lax.broadcasted_iota(jnp.int32, la.shape, 0) < valid, la, 0.0)
        b_gla = jnp.dot(tril, la, precision=lax.Precision.HIGHEST, preferred_element_type=F32)
        for p in range(4):
            pp = p % 2
            if p < 2:
                q = z_ref[rows, Z_RQ + LANE * pp:Z_RQ + LANE * (pp + 1)]
                k = z_ref[rows, Z_RK + LANE * pp:Z_RK + LANE * (pp + 1)] * DK_A ** -0.5
                v = z_ref[rows, Z_RV + 2 * LANE * pp:Z_RV + 2 * LANE * (pp + 1)]
                b = steps_done * jnp.where(lo, _LOG_GAMMA[2 * pp], _LOG_GAMMA[2 * pp + 1])
                z_gate, gain_ref, o_col = Z_RG, gr_ref, 0
            else:
                q = z_ref[rows, Z_GQ + LANE * pp:Z_GQ + LANE * (pp + 1)] * DK_A ** -0.5
                k = z_ref[rows, Z_GK + LANE * pp:Z_GK + LANE * (pp + 1)]
                v = z_ref[rows, Z_GV + 2 * LANE * pp:Z_GV + 2 * LANE * (pp + 1)]
                b = b_gla[:, LANE * pp:LANE * (pp + 1)]
                z_gate, gain_ref, o_col = Z_GG, gg_ref, H_A * DV_A
            b_last = b[C - 1:C, :]
            qd = q * jnp.exp(b)
            k_in = (k * jnp.exp(-b)).astype(BF16)
            k_out = (k * jnp.exp(b_last - b)).astype(BF16)
            vb = v.astype(BF16)
            q2 = jnp.concatenate([jnp.where(lo, qd, 0.0), jnp.where(lo, 0.0, qd)], axis=0).astype(BF16)
            att = lax.dot_general(q2, k_in, _NT, preferred_element_type=F32)
            att = jnp.where(causal2, att, 0.0).astype(BF16)
            st = st_ref[p]
            o = lax.dot_general(qd.astype(BF16), st.astype(BF16), _NT, preferred_element_type=F32)
            o = o + jnp.concatenate([_dot(att[:C], vb[:, :DV_A]), _dot(att[C:], vb[:, DV_A:])], axis=1)
            delta_t = lax.dot_general(vb, k_out, _TN, preferred_element_type=F32)
            st_ref[p] = st * jnp.exp(b_last) + jnp.where(blockdiag, delta_t, 0.0)
            for hh in range(2):
                head = 2 * pp + hh
                oh = _head_rms(o[:, DV_A * hh:DV_A * (hh + 1)])
                gate = z_ref[rows, z_gate + DV_A * head:z_gate + DV_A * (head + 1)]
                gain = gain_ref[:, DV_A * head:DV_A * (head + 1)]
                o_ref[rows, o_col + DV_A * head:o_col + DV_A * (head + 1)] = (oh * gain * _silu(gate)).astype(BF16)

    @pl.when(step == pl.num_programs(1) - 1)
    def _():
        for p in range(4):
            s = st_ref[p].T
            out = sr_ref if p < 2 else sg_ref
            out[2 * (p % 2)] = s[:DK_A, :DV_A]
            out[2 * (p % 2) + 1] = s[DK_A:, DV_A:]


def _even_mixer(z, s_ret, s_gla, wa2, ba, g_ret, g_gla, valid):
    bsz, L, _ = z.shape
    rows = min(L, 4 * CHUNK)
    state = pl.BlockSpec((None, H_A, DK_A, DV_A), lambda b, s: (b, 0, 0, 0))
    st_shape = jax.ShapeDtypeStruct((bsz, H_A, DK_A, DV_A), F32)
    return pl.pallas_call(
        functools.partial(_even_kernel, n_chunks=rows // CHUNK, valid=valid),
        out_shape=[jax.ShapeDtypeStruct((bsz, L, D_MODEL), BF16), st_shape, st_shape],
        grid=(bsz, L // rows),
        in_specs=[pl.BlockSpec((None, rows, Z_W), lambda b, s: (b, s, 0)), state, state,
                  _resident((LANE, 2 * LANE)), _resident((1, 2 * LANE)),
                  _resident((1, H_A * DV_A)), _resident((1, H_A * DV_A))],
        out_specs=[pl.BlockSpec((None, rows, D_MODEL), lambda b, s: (b, s, 0)), state, state],
        scratch_shapes=[pltpu.VMEM((4, 2 * DV_A, 2 * DK_A), F32)],
        compiler_params=_params(("arbitrary", "arbitrary"), 32), name="even_mixer")(
            z, s_ret, s_gla, wa2, ba, g_ret, g_gla)


def _diff_lambda(lq1_ref, lk1_ref, lq2_ref, lk2_ref, lam_init):
    a = jnp.exp(jnp.sum(lq1_ref[...] * lk1_ref[...], axis=-1, keepdims=True))
    b = jnp.exp(jnp.sum(lq2_ref[...] * lk2_ref[...], axis=-1, keepdims=True))
    return a - b + lam_init


def _two_maps(q):
    lane = lax.broadcasted_iota(jnp.int32, q.shape, 1)
    lo = lane < DH_C
    return jnp.concatenate([jnp.where(lo, q, 0.0), jnp.where(lo, 0.0, q)], axis=0)


def _dprompt_kernel(slope_ref, q_ref, k_ref, v_ref, kpos_ref, lq1_ref, lk1_ref, lq2_ref, lk2_ref, gain_ref,
                    o_ref, kaug_ref, vt_ref, s_ref, *, tq, lam_init):
    h = pl.program_id(1)
    nk = vt_ref.shape[0]
    kaug_ref[:, :LANE] = k_ref[...]
    kaug_ref[:, LANE:] = kpos_ref[...]
    for j in range(nk):
        vt_ref[j] = v_ref[j * tq:(j + 1) * tq, :].astype(F32).T.astype(BF16)

    key = lax.broadcasted_iota(jnp.int32, (tq, 2 * tq), 0)
    qry = lax.broadcasted_iota(jnp.int32, (tq, 2 * tq), 1)
    causal = key <= jnp.where(qry >= tq, qry - tq, qry)
    lam = _diff_lambda(lq1_ref, lk1_ref, lq2_ref, lk2_ref, lam_init)

    def attend(qb, slot):
        n = qb + 1
        rows = slice(qb * tq, (qb + 1) * tq)
        q2 = _two_maps(q_ref[rows, :].astype(F32) * DH_C ** -0.5)
        lane = lax.broadcasted_iota(jnp.int32, q2.shape, 1)
        q2aug = jnp.concatenate([q2, jnp.where(lane < 2, slope_ref[h], 0.0)], axis=1).astype(BF16)
        s_all = lax.dot_general(kaug_ref[:n * tq, :], q2aug, _NT, preferred_element_type=F32)
        m = None
        for j in range(n):
            s = s_all[j * tq:(j + 1) * tq]
            if j == n - 1:
                s = jnp.where(causal, s, NEG)
            s_ref[slot + j] = s
            mj = jnp.max(s, axis=0, keepdims=True)
            m = mj if m is None else jnp.maximum(m, mj)
        l = None
        acc = None
        for j in range(n):
            p = jnp.exp(s_ref[slot + j] - m)
            lj = jnp.sum(p, axis=0, keepdims=True)
            aj = _dot(vt_ref[j], p.astype(BF16))
            l = lj if l is None else l + lj
            acc = aj if acc is None else acc + aj
        o = acc / l
        od = (o[:, :tq] - lam * o[:, tq:]).T
        o_ref[rows, :] = (_head_rms(od) * gain_ref[...] * (1.0 - lam_init)).astype(BF16)

    slot = 0
    for qb in range(nk):
        attend(qb, slot)
        slot += qb + 1


def _diff_prompt(q, k, v, slopes, lams, gain, lam_init):
    bsz, L, _ = q.shape
    tq = 256
    nq = L // tq
    hd = 2 * DH_C
    pos = np.arange(L)
    kpos = np.zeros((L, LANE), np.float32)
    kpos[:, 0] = pos // 16 * 16
    kpos[:, 1] = pos % 16
    assert L <= 4096 and L % tq == 0
    blk = pl.BlockSpec((None, L, hd), lambda b, h: (b, 0, h))
    vec = lambda n: pl.BlockSpec((1, n), lambda b, h: (0, 0))
    return pl.pallas_call(
        functools.partial(_dprompt_kernel, tq=tq, lam_init=lam_init),
        out_shape=jax.ShapeDtypeStruct((bsz, L, D_MODEL), BF16),
        grid=(bsz, H_C),
        in_specs=[pl.BlockSpec(memory_space=pltpu.SMEM), blk, blk, blk,
                  pl.BlockSpec((L, LANE), lambda b, h: (0, 0)),
                  vec(DH_C), vec(DH_C), vec(DH_C), vec(DH_C), vec(hd)],
        out_specs=blk,
        scratch_shapes=[pltpu.VMEM((L, 2 * LANE), BF16), pltpu.VMEM((nq, hd, tq), BF16),
                        pltpu.VMEM((nq * (nq + 1) // 2, tq, 2 * tq), F32)],
        compiler_params=_params(("arbitrary", "arbitrary"), 40), name="diff_prompt")(
            slopes, q, k, v, jnp.asarray(kpos, BF16), *lams, gain)


PAGES_PER_STEP = 8
RING = 3


def _row_max(s):
    m = s[:, :LANE]
    for j in range(1, s.shape[1] // LANE):
        m = jnp.maximum(m, s[:, j * LANE:(j + 1) * LANE])
    return jnp.broadcast_to(jnp.max(m, axis=1, keepdims=True), m.shape)


def _row_sum(p):
    return jnp.broadcast_to(jnp.sum(p, axis=1, keepdims=True), p.shape)


def _dsample_kernel(pt_ref, q_ref, kn_ref, vn_ref, t1_ref, t1n_ref, slope_ref, lq1_ref, lk1_ref, lq2_ref,
                    lk2_ref, gain_ref, ck_hbm, cv_hbm, o_ref, qall_ref, m_ref, corr_ref, l_ref, acc_ref, sa_ref,
                    sb_ref, p_ref, kbuf, vbuf, ksem, vsem, *, lam_init, n_pages, layer):
    P = PAGES_PER_STEP
    b = pl.program_id(0)
    j = pl.program_id(1)
    n_batch = pl.num_programs(0)
    n_steps = n_pages // P
    T = q_ref.shape[0]
    hd = 2 * DH_C
    n_blk = PAGE * H_C // LANE
    wide = 2 * LANE

    def page_copies(hbm, buf, sem, row, step):
        slot = (row * n_steps + step) % RING
        return [pltpu.make_async_copy(hbm.at[layer, pt_ref[row, step * P + i]], buf.at[slot, i], sem.at[slot])
                for i in range(P)]

    def start_for(row, jj):
        @pl.when(jnp.logical_and(row < n_batch, jj < n_steps))
        def _():
            for c in page_copies(ck_hbm, kbuf, ksem, row, jj):
                c.start()

        @pl.when(jnp.logical_and(row < n_batch, jj >= 1))
        def _():
            for c in page_copies(cv_hbm, vbuf, vsem, row, jj - 1):
                c.start()

    @pl.when(jnp.logical_and(b == 0, j == 0))
    def _():
        start_for(b, j)
        start_for(b, j + 1)

    ahead = b * (n_steps + 1) + j + 2
    start_for(ahead // (n_steps + 1), ahead % (n_steps + 1))

    @pl.when(j == 0)
    def _():
        for h in range(H_C):
            qh = q_ref[:, hd * h:hd * (h + 1)].astype(F32) * DH_C ** -0.5
            qall_ref[2 * T * h:2 * T * (h + 1), :] = _two_maps(qh).astype(BF16)
        m_ref[...] = jnp.full_like(m_ref, NEG)
        corr_ref[...] = jnp.ones_like(corr_ref)
        l_ref[...] = jnp.zeros_like(l_ref)
        acc_ref[...] = jnp.zeros_like(acc_ref)

    qall = qall_ref[...]
    page_off = lambda step, i: slope_ref[...] * ((step * P + i) * PAGE).astype(F32)

    def score(s_w, m_prev):
        for c in page_copies(ck_hbm, kbuf, ksem, b, j):
            c.wait()
        slot = (b * n_steps + j) % RING
        rm = None
        for i in range(P):
            pm = None
            for blk in range(PAGE * H_C // wide):
                keys = slice(blk * wide // H_C, (blk + 1) * wide // H_C)
                k2 = kbuf[slot, i, keys].reshape(wide, hd).astype(BF16)
                s = lax.dot_general(qall, k2, _NT, preferred_element_type=F32) + t1_ref[:, blk * wide:(blk + 1) * wide]
                s_w[i, :, blk * wide:(blk + 1) * wide] = s
                mb = jnp.maximum(s[:, :LANE], s[:, LANE:])
                pm = mb if pm is None else jnp.maximum(pm, mb)
            r = _row_max(pm) + page_off(j, i)
            rm = r if rm is None else jnp.maximum(rm, r)
        m_new = jnp.maximum(m_prev, rm)
        corr_ref[...] = jnp.exp(m_prev - m_new)
        m_ref[...] = m_new

    def apply(s_r, m_prev, corr_prev):
        for c in page_copies(cv_hbm, vbuf, vsem, b, j - 1):
            c.wait()
        slot = (b * n_steps + j - 1) % RING
        lsum = jnp.zeros((LANE, LANE), F32)
        pv = None
        for i in range(P):
            sub = m_prev - page_off(j - 1, i)
            for blk in range(n_blk):
                p = jnp.exp(s_r[i, :, blk * LANE:(blk + 1) * LANE] - sub)
                lsum = lsum + p
                p_ref[i, :, blk * LANE:(blk + 1) * LANE] = p.astype(BF16)
            v2 = vbuf[slot, i].reshape(PAGE * H_C, hd).astype(BF16)
            d = _dot(p_ref[i], v2)
            pv = d if pv is None else pv + d
        l_ref[...] = l_ref[...] * corr_prev + _row_sum(lsum)
        acc_ref[...] = acc_ref[...] * corr_prev + pv

    def both(s_w, s_r):
        m_prev = m_ref[...]
        corr_prev = corr_ref[...]
        score(s_w, m_prev)
        apply(s_r, m_prev, corr_prev)

    middle = jnp.logical_and(j > 0, j < n_steps)
    pl.when(j == 0)(lambda: score(sa_ref, m_ref[...]))
    pl.when(jnp.logical_and(middle, j % 2 == 0))(functools.partial(both, sa_ref, sb_ref))
    pl.when(jnp.logical_and(middle, j % 2 == 1))(functools.partial(both, sb_ref, sa_ref))
    last_scores = sb_ref if n_steps % 2 == 0 else sa_ref
    pl.when(j == n_steps)(lambda: apply(last_scores, m_ref[...], corr_ref[...]))

    @pl.when(j == n_steps)
    def _():
        s = lax.dot_general(qall, kn_ref[...], _NT, preferred_element_type=F32) + t1n_ref[...]
        off = slope_ref[...] * float(n_pages * PAGE)
        m_prev = m_ref[...]
        m_fin = jnp.maximum(m_prev, _row_max(s) + off)
        c = jnp.exp(m_prev - m_fin)
        p = jnp.exp(s - (m_fin - off))
        l = l_ref[...] * c + _row_sum(p)
        o = (acc_ref[...] * c + _dot(p.astype(BF16), vn_ref[...])) / l
        lam = _diff_lambda(lq1_ref, lk1_ref, lq2_ref, lk2_ref, lam_init)
        for h in range(H_C):
            od = o[2 * T * h:2 * T * h + T] - lam * o[2 * T * h + T:2 * T * (h + 1)]
            o_ref[:, hd * h:hd * (h + 1)] = (_head_rms(od) * gain_ref[...] * (1.0 - lam_init)).astype(BF16)


def _diff_sample(q, k_new, v_new, cache_k, cache_v, layer, page_table, lams, gain, lam_init):
    bsz, T, _ = q.shape
    n_pages = page_table.shape[1]
    hd = 2 * DH_C
    P = PAGES_PER_STEP
    n_rows = 2 * T * H_C
    assert n_rows == LANE and n_pages % P == 0
    c = np.arange(n_rows)
    c_h, c_t = c // (2 * T), c % T
    slope_c = 2.0 ** (-8.0 * (c_h + 1) / H_C)
    r = np.arange(PAGE * H_C)
    r_key, r_h = r // H_C, r % H_C
    same_head = r_h[None, :] == c_h[:, None]
    t1 = np.where(same_head, slope_c[:, None] * r_key[None, :], NEG)
    new_ok = same_head[:, :LANE] & (r_key[None, :LANE] <= c_t[:, None]) & (r_key[None, :LANE] < T)
    t1n = np.where(new_ok, slope_c[:, None] * r_key[None, :LANE], NEG)
    slope_tile = np.broadcast_to(slope_c[:, None], (n_rows, LANE))
    pad = lambda a: jnp.pad(a.reshape(bsz, T * H_C, hd), ((0, 0), (0, LANE - T * H_C), (0, 0)))

    n_steps = n_pages // P
    const = lambda shape: pl.BlockSpec(shape, lambda b, j, pt: (0,) * len(shape))
    per_b = lambda shape: pl.BlockSpec((None,) + shape, lambda b, j, pt: (b,) + (0,) * len(shape))
    scores = (P, n_rows, PAGE * H_C)
    stat = pltpu.VMEM((n_rows, LANE), F32)
    grid_spec = pltpu.PrefetchScalarGridSpec(
        num_scalar_prefetch=1, grid=(bsz, n_steps + 1),
        in_specs=[per_b((T, D_MODEL)), per_b((LANE, hd)), per_b((LANE, hd)),
                  const((n_rows, PAGE * H_C)), const((n_rows, LANE)), const((n_rows, LANE)),
                  const((1, DH_C)), const((1, DH_C)), const((1, DH_C)), const((1, DH_C)), const((1, hd))]
                 + [pl.BlockSpec(memory_space=pl.ANY)] * 2,
        out_specs=per_b((T, D_MODEL)),
        scratch_shapes=[pltpu.VMEM((n_rows, hd), BF16), stat, stat, stat, pltpu.VMEM((n_rows, hd), F32),
                        pltpu.VMEM(scores, F32), pltpu.VMEM(scores, F32), pltpu.VMEM(scores, BF16),
                        pltpu.VMEM((RING, P, PAGE, H_C, hd), F32), pltpu.VMEM((RING, P, PAGE, H_C, hd), F32),
                        pltpu.SemaphoreType.DMA((RING,)), pltpu.SemaphoreType.DMA((RING,))])
    return pl.pallas_call(
        functools.partial(_dsample_kernel, lam_init=lam_init, n_pages=n_pages, layer=layer),
        out_shape=jax.ShapeDtypeStruct((bsz, T, D_MODEL), BF16), grid_spec=grid_spec,
        compiler_params=_params(("arbitrary", "arbitrary"), 48), name="diff_sample")(
            page_table, q, pad(k_new), pad(v_new), jnp.asarray(t1, F32), jnp.asarray(t1n, F32),
            jnp.asarray(slope_tile, F32), *lams, gain, cache_k, cache_v)


def _cross_kernel(x_ref, g_ref, wq_ref, mk_ref, mv_ref, wo_ref, o_ref):
    G, T, _ = x_ref.shape
    x = x_ref[...].reshape(G * T, D_MODEL)
    q = _dot(_rms(x, g_ref[...]).astype(BF16), wq_ref[...])
    rows = []
    for g in range(G):
        heads = []
        for h in range(H_X):
            cols = slice(DH_X * h, DH_X * (h + 1))
            qh = q[g * T:(g + 1) * T, cols].astype(BF16)
            s = lax.dot_general(qh, mk_ref[g, :, cols].astype(BF16), _NT, preferred_element_type=F32) * DH_X ** -0.5
            p = jnp.exp(s - jnp.max(s, axis=-1, keepdims=True))
            p = p / jnp.sum(p, axis=-1, keepdims=True)
            heads.append(_dot(p.astype(BF16), mv_ref[g, :, cols].astype(BF16)))
        rows.append(jnp.concatenate(heads, axis=1))
    o = jnp.concatenate(rows, axis=0).astype(BF16)
    o_ref[...] = (x + _dot(o, wo_ref[...])).reshape(G, T, D_MODEL)


def _cross(x, gains, wq, mk, mv, wo, li, group, rows):
    bsz, L, _ = x.shape
    xs = pl.BlockSpec((group, rows, D_MODEL), lambda b, t: (b, t, 0))
    mem = pl.BlockSpec((None, group, N_MEM, D_MODEL), lambda b, t: (li, b, 0, 0))
    return pl.pallas_call(
        _cross_kernel, out_shape=jax.ShapeDtypeStruct(x.shape, F32), grid=(bsz // group, L // rows),
        in_specs=[xs, _layer(gains, li), _layer(wq, li), mem, mem, _layer(wo, li)],
        out_specs=xs, compiler_params=_params(("arbitrary", "arbitrary"), 48), name="cross")(
            x, gains, wq, mk, mv, wo)


def _cross_cache_kernel(x_ref, g_ref, wq_ref, mk_ref, mv_ref, wo_ref, o_ref):
    G, T, _ = x_ref.shape
    n_kv = N_MEM * H_X
    x = x_ref[...].reshape(G * T, D_MODEL)
    q = _dot(_rms(x, g_ref[...]).astype(BF16), wq_ref[...])
    same = (lax.broadcasted_iota(jnp.int32, (H_X * T, n_kv), 0) // T
            == lax.broadcasted_iota(jnp.int32, (H_X * T, n_kv), 1) % H_X)
    rows = []
    for g in range(G):
        qg = q[g * T:(g + 1) * T]
        q4 = jnp.concatenate([qg[:, DH_X * h:DH_X * (h + 1)] for h in range(H_X)], axis=0).astype(BF16)
        k2 = mk_ref[g].reshape(n_kv, DH_X).astype(BF16)
        v2 = mv_ref[g].reshape(n_kv, DH_X).astype(BF16)
        s = lax.dot_general(q4, k2, _NT, preferred_element_type=F32) * DH_X ** -0.5
        s = jnp.where(same, s, NEG)
        p = jnp.exp(s - jnp.max(s, axis=-1, keepdims=True))
        p = p / jnp.sum(p, axis=-1, keepdims=True)
        o4 = _dot(p.astype(BF16), v2)
        rows.append(jnp.concatenate([o4[T * h:T * (h + 1)] for h in range(H_X)], axis=1))
    o = jnp.concatenate(rows, axis=0).astype(BF16)
    o_ref[...] = (x + _dot(o, wo_ref[...])).reshape(G, T, D_MODEL)


def _cross_cache(x, gains, wq, cache_k, cache_v, wo, li, group):
    bsz, T, _ = x.shape
    xs = pl.BlockSpec((group, T, D_MODEL), lambda b: (b, 0, 0))
    mem = pl.BlockSpec((None, group, N_MEM, H_X, DH_X), lambda b: (li, b, 0, 0, 0))
    return pl.pallas_call(
        _cross_cache_kernel, out_shape=jax.ShapeDtypeStruct(x.shape, F32), grid=(bsz // group,),
        in_specs=[xs, _layer(gains, li), _layer(wq, li), mem, mem, _layer(wo, li)],
        out_specs=xs, compiler_params=_params(("arbitrary",), 56), name="cross_cache")(
            x, gains, wq, cache_k, cache_v, wo)


def kernel(x_prompt, x_sample, state_ret, state_gla, cache_k, cache_v, cache_mem_k, cache_mem_v, page_table, mem_prompt, norm_ffn1, ffn1_wg, ffn1_wu, ffn1_wd, norm_mix, w_in_even, gla_wa2, gla_ba, ret_gain, gla_gain, w_in_odd, lam_q1, lam_k1, lam_q2, lam_k2, diff_gain, w_mix_out, norm_x, x_wq, x_wk, x_wv, x_wo, norm_ffn2, ffn2_wg, ffn2_wu, ffn2_wd, final_norm):
    bp, lp, _ = x_prompt.shape
    bs, ls, _ = x_sample.shape
    depth = norm_ffn1.shape[0]
    bf = lambda w: w.astype(BF16)
    xp = x_prompt.reshape(bp * lp, D_MODEL)
    xs = x_sample.reshape(bs * ls, D_MODEL)
    slopes = jnp.asarray([2.0 ** (-8.0 * (h + 1) / H_C) for h in range(H_C)], F32)
    ret_p, ret_s, gla_p, gla_s, kr_p, vr_p, kr_s, vr_s = ([] for _ in range(8))
    rows = lambda g: g.reshape(g.shape[0], 1, g.shape[1])
    g_ffn1, g_mix, g_x, g_ffn2 = rows(norm_ffn1), rows(norm_mix), rows(norm_x), rows(norm_ffn2)
    w1 = (bf(ffn1_wg), bf(ffn1_wu), bf(ffn1_wd))
    w2 = (bf(ffn2_wg), bf(ffn2_wu), bf(ffn2_wd))
    w_out, wq, wo = bf(w_mix_out), bf(x_wq), bf(x_wo)
    w_even = bf(jnp.pad(w_in_even, ((0, 0), (0, 0), (0, Z_W - w_in_even.shape[2]))))
    w_odd = bf(w_in_odd)
    w_kv = bf(jnp.concatenate([w for li in range(depth) for w in (x_wk[li], x_wv[li])], axis=1))
    mk, mv, mk4, mv4 = _mem_proj(mem_prompt.reshape(bp * N_MEM, D_MODEL), w_kv, depth)
    mk = mk.reshape(depth, bp, N_MEM, D_MODEL)
    mv = mv.reshape(depth, bp, N_MEM, D_MODEL)

    for li in range(depth):
        xp = _ffn(xp, g_ffn1, *w1, li)
        xs = _ffn(xs, g_ffn1, *w1, li)
        if li % 2 == 0:
            e = li // 2
            wa2 = bf(jnp.pad(gla_wa2[e], ((0, LANE - GLA_RANK), (0, 0))))
            prm = (wa2, gla_ba[e].reshape(1, -1), ret_gain[e].reshape(1, -1), gla_gain[e].reshape(1, -1))
            (zp,) = _proj(xp, g_mix, li, w_even, e, ((0, Z_W, F32),))
            (zs,) = _proj(xs, g_mix, li, w_even, e, ((0, Z_W, F32),))
            zero = jnp.zeros((bp, H_A, DK_A, DV_A), F32)
            op, sr, sg = _even_mixer(zp.reshape(bp, lp, Z_W), zero, zero, *prm, valid=CHUNK)
            ret_p.append(sr); gla_p.append(sg)
            zs = jnp.pad(zs.reshape(bs, ls, Z_W), ((0, 0), (0, CHUNK - ls), (0, 0)))
            os_, sr, sg = _even_mixer(zs, state_ret[e], state_gla[e], *prm, valid=ls)
            ret_s.append(sr); gla_s.append(sg)
            os_ = os_[:, :ls]
        else:
            o = li // 2
            lam_init = 0.8 - 0.6 * math.exp(-0.3 * li)
            lams = tuple(a[o].reshape(1, DH_C) for a in (lam_q1, lam_k1, lam_q2, lam_k2))
            gain = diff_gain[o].reshape(1, 2 * DH_C)
            hd = 2 * DH_C
            outs = ((0, D_MODEL, BF16), (D_MODEL, D_MODEL, BF16), (2 * D_MODEL, D_MODEL, BF16),
                    (D_MODEL, D_MODEL, hd), (2 * D_MODEL, D_MODEL, hd))
            qp, kp, vp, kp4, vp4 = _proj(xp, g_mix, li, w_odd, o, outs)
            qs, ks, vs, ks4, vs4 = _proj(xs, g_mix, li, w_odd, o, outs)
            shp = lambda a: a.reshape(bp, lp, D_MODEL)
            shs = lambda a: a.reshape(bs, ls, D_MODEL)
            op = _diff_prompt(shp(qp), shp(kp), shp(vp), slopes, lams, gain, lam_init)
            os_ = _diff_sample(shs(qs), shs(ks), shs(vs), cache_k, cache_v, o, page_table, lams, gain, lam_init)
            kr_p.append(kp4.reshape(bp, lp, H_C, hd)); vr_p.append(vp4.reshape(bp, lp, H_C, hd))
            kr_s.append(ks4.reshape(bs, ls, H_C, hd)); vr_s.append(vs4.reshape(bs, ls, H_C, hd))
        xp = _outproj(op.reshape(bp * lp, D_MODEL), w_out, li, xp)
        xs = _outproj(os_.reshape(bs * ls, D_MODEL), w_out, li, xs)
        xp = _cross(xp.reshape(bp, lp, D_MODEL), g_x, wq, mk, mv, wo, li, group=1, rows=512).reshape(bp * lp, D_MODEL)
        xs = _cross_cache(xs.reshape(bs, ls, D_MODEL), g_x, wq, cache_mem_k, cache_mem_v, wo, li,
                          group=4).reshape(bs * ls, D_MODEL)
        fg = final_norm if li == depth - 1 else None
        xp = _ffn(xp, g_ffn2, *w2, li, final_gain=fg)
        xs = _ffn(xs, g_ffn2, *w2, li, final_gain=fg)

    return (xp.reshape(bp, lp, D_MODEL), xs.reshape(bs, ls, D_MODEL),
            jnp.stack(ret_p), jnp.stack(ret_s), jnp.stack(gla_p), jnp.stack(gla_s),
            jnp.stack(kr_p), jnp.stack(vr_p), jnp.stack(kr_s), jnp.stack(vr_s),
            mk4.reshape(depth, bp, N_MEM, H_X, DH_X), mv4.reshape(depth, bp, N_MEM, H_X, DH_X))
```

```python
import functools
import math

import numpy as np
import jax
import jax.numpy as jnp
from jax import lax
from jax.experimental import pallas as pl
from jax.experimental.pallas import tpu as pltpu

F32 = jnp.float32
BF16 = jnp.bfloat16

D_MODEL = 1024
D_FF = 2816
EPS = 1e-6
CHUNK = 64
H_A, DK_A, DV_A = 4, 64, 128
GLA_RANK = 16
GLA_TAU = 16.0
H_C, DH_C = 8, 64
H_X, DH_X = 4, 256
N_MEM = 256
PAGE = 128
NEG = -1e30

LANE = 128
FF_CHUNK = 256
MIB = 1024 * 1024

Z_RQ, Z_RK, Z_RV, Z_RG, Z_GQ, Z_GK, Z_GV, Z_GG, Z_GLR = 0, 256, 512, 1024, 1536, 1792, 2048, 2560, 3072
Z_W = 3200

_NT = (((1,), (1,)), ((), ()))
_TN = (((0,), (0,)), ((), ()))


def _params(sem, vmem_mib):
    return pltpu.CompilerParams(dimension_semantics=sem, vmem_limit_bytes=vmem_mib * MIB)


def _resident(shape):
    nd = len(shape)
    return pl.BlockSpec(shape, lambda *_: (0,) * nd, pipeline_mode=pl.Buffered(1))


def _layer(stacked, li):
    shape = stacked.shape[1:]
    return pl.BlockSpec((None,) + shape, lambda *_: (li,) + (0,) * len(shape), pipeline_mode=pl.Buffered(1))


def _rms(x, g):
    return x * lax.rsqrt(jnp.mean(x * x, axis=-1, keepdims=True) + EPS) * g


def _head_rms(o):
    return o * lax.rsqrt(jnp.mean(o * o, axis=-1, keepdims=True) + EPS)


def _silu(x):
    return x * jax.nn.sigmoid(x)


def _dot(a, b):
    return jnp.dot(a, b, preferred_element_type=F32)


def _ffn_kernel(x_ref, g_ref, wg_ref, wu_ref, wd_ref, *rest, final):
    if final:
        fg_ref, o_ref, a_ref = rest
    else:
        o_ref, a_ref = rest
    x = x_ref[...]
    h = _rms(x, g_ref[...]).astype(BF16)
    for c in range(D_FF // FF_CHUNK):
        sl = slice(c * FF_CHUNK, (c + 1) * FF_CHUNK)
        g = _dot(h, wg_ref[:, sl])
        u = _dot(h, wu_ref[:, sl])
        a_ref[:, sl] = (_silu(g) * u).astype(BF16)
    y = x + 0.5 * _dot(a_ref[...], wd_ref[...])
    if final:
        y = _rms(y, fg_ref[...])
    o_ref[...] = y


def _ffn(x, gains, wg, wu, wd, li, final_gain=None):
    n = x.shape[0]
    tm = min(1024, n)
    final = final_gain is not None
    row = pl.BlockSpec((tm, D_MODEL), lambda i: (i, 0))
    in_specs = [row] + [_layer(a, li) for a in (gains, wg, wu, wd)]
    args = [x, gains, wg, wu, wd]
    if final:
        in_specs.append(_resident((1, D_MODEL)))
        args.append(final_gain.reshape(1, D_MODEL))
    return pl.pallas_call(
        functools.partial(_ffn_kernel, final=final),
        out_shape=jax.ShapeDtypeStruct((n, D_MODEL), F32),
        grid=(n // tm,), in_specs=in_specs, out_specs=row,
        scratch_shapes=[pltpu.VMEM((tm, D_FF), BF16)],
        compiler_params=_params(("arbitrary",), 56), name="ffn")(*args)


def _heads_store(o_ref, y, width):
    for hh in range(y.shape[1] // width):
        o_ref[:, hh, :] = y[:, hh * width:(hh + 1) * width]


def _proj_kernel(x_ref, g_ref, w_ref, *o_refs, outs):
    h = _rms(x_ref[...], g_ref[...]).astype(BF16)
    done = {}
    for o_ref, (off, n, kind) in zip(o_refs, outs):
        if (off, n) not in done:
            done[(off, n)] = _dot(h, w_ref[:, off:off + n])
        y = done[(off, n)]
        if isinstance(kind, int):
            _heads_store(o_ref, y, kind)
        else:
            o_ref[...] = y.astype(o_ref.dtype)


def _proj(x, gains, li, w, wi, outs):
    n, k = x.shape
    tm = min(512, n)
    in_specs = [pl.BlockSpec((tm, k), lambda i: (i, 0)), _layer(gains, li), _layer(w, wi)]
    args = [x, gains, w]
    out_shape, out_specs = [], []
    for _, width, kind in outs:
        if isinstance(kind, int):
            out_shape.append(jax.ShapeDtypeStruct((n, width // kind, kind), F32))
            out_specs.append(pl.BlockSpec((tm, width // kind, kind), lambda i: (i, 0, 0)))
        else:
            out_shape.append(jax.ShapeDtypeStruct((n, width), kind))
            out_specs.append(pl.BlockSpec((tm, width), lambda i: (i, 0)))
    return pl.pallas_call(
        functools.partial(_proj_kernel, outs=outs),
        out_shape=out_shape, grid=(n // tm,), in_specs=in_specs, out_specs=out_specs,
        compiler_params=_params(("arbitrary",), 40), name="proj")(*args)


def _mem_kernel(x_ref, w_ref, mk_ref, mv_ref, mk4_ref, mv4_ref):
    h = x_ref[...].astype(BF16)
    for l in range(mk_ref.shape[0]):
        for t, (o2, o4) in enumerate(((mk_ref, mk4_ref), (mv_ref, mv4_ref))):
            y = _dot(h, w_ref[:, (2 * l + t) * D_MODEL:(2 * l + t + 1) * D_MODEL])
            o2[l] = y.astype(BF16)
            _heads_store(o4.at[l], y, DH_X)


def _mem_proj(mem, w, depth):
    n = mem.shape[0]
    tm = min(512, n)
    flat = pl.BlockSpec((depth, tm, D_MODEL), lambda i: (0, i, 0))
    heads = pl.BlockSpec((depth, tm, H_X, DH_X), lambda i: (0, i, 0, 0))
    return pl.pallas_call(
        _mem_kernel,
        out_shape=[jax.ShapeDtypeStruct((depth, n, D_MODEL), BF16)] * 2
        + [jax.ShapeDtypeStruct((depth, n, H_X, DH_X), F32)] * 2,
        grid=(n // tm,), in_specs=[pl.BlockSpec((tm, D_MODEL), lambda i: (i, 0)), _resident(w.shape)],
        out_specs=[flat, flat, heads, heads],
        compiler_params=_params(("arbitrary",), 48), name="mem_proj")(mem, w)


def _outproj_kernel(a_ref, w_ref, r_ref, o_ref):
    o_ref[...] = r_ref[...] + _dot(a_ref[...], w_ref[...])


def _outproj(a, w, li, res):
    n, k = a.shape
    tm = min(512, n)
    return pl.pallas_call(
        _outproj_kernel, out_shape=jax.ShapeDtypeStruct((n, D_MODEL), F32), grid=(n // tm,),
        in_specs=[pl.BlockSpec((tm, k), lambda i: (i, 0)), _layer(w, li),
                  pl.BlockSpec((tm, D_MODEL), lambda i: (i, 0))],
        out_specs=pl.BlockSpec((tm, D_MODEL), lambda i: (i, 0)),
        compiler_params=_params(("arbitrary",), 32), name="outproj")(a, w, res)


_LOG_GAMMA = [math.log1p(-2.0 ** (-5.0 - h)) for h in range(H_A)]


def _even_kernel(z_ref, sr0_ref, sg0_ref, wa2_ref, ba_ref, gr_ref, gg_ref, o_ref, sr_ref, sg_ref, st_ref,
                 *, n_chunks, valid):
    C = CHUNK
    step = pl.program_id(1)

    def pair_state(a, b):
        zero = jnp.zeros((DK_A, DV_A), F32)
        s = jnp.concatenate([jnp.concatenate([a, zero], axis=1), jnp.concatenate([zero, b], axis=1)], axis=0)
        return s.T

    @pl.when(step == 0)
    def _():
        for pp in range(2):
            st_ref[pp] = pair_state(sr0_ref[2 * pp], sr0_ref[2 * pp + 1])
            st_ref[2 + pp] = pair_state(sg0_ref[2 * pp], sg0_ref[2 * pp + 1])

    row = lax.broadcasted_iota(jnp.int32, (C, LANE), 0)
    lane = lax.broadcasted_iota(jnp.int32, (C, LANE), 1)
    lo = lane < DK_A
    r2 = lax.broadcasted_iota(jnp.int32, (2 * C, C), 0)
    c2 = lax.broadcasted_iota(jnp.int32, (2 * C, C), 1)
    causal2 = jnp.where(r2 >= C, r2 - C, r2) >= c2
    tr = lax.broadcasted_iota(jnp.int32, (C, C), 0)
    tc = lax.broadcasted_iota(jnp.int32, (C, C), 1)
    tril = (tr >= tc).astype(F32)
    sr_ = lax.broadcasted_iota(jnp.int32, (2 * DV_A, 2 * DK_A), 0)
    sc_ = lax.broadcasted_iota(jnp.int32, (2 * DV_A, 2 * DK_A), 1)
    blockdiag = (sr_ >= DV_A) == (sc_ >= DK_A)
    steps_done = jnp.minimum(row + 1, valid).astype(F32)

    for ci in range(n_chunks):
        rows = slice(ci * C, (ci + 1) * C)
        glr = z_ref[rows, Z_GLR:Z_GLR + LANE].astype(BF16)
        xg = _dot(glr, wa2_ref[...]) + ba_ref[...]
        la = (jnp.minimum(xg, 0.0) - jnp.log1p(jnp.exp(-jnp.abs(xg)))) / GLA_TAU
        if valid < C:
            la = jnp.where(lax.broadcasted_iota(jnp.int32, la.shape, 0) < valid, la, 0.0)
        b_gla = jnp.dot(tril, la, precision=lax.Precision.HIGHEST, preferred_element_type=F32)
        for p in range(4):
            pp = p % 2
            if p < 2:
                q = z_ref[rows, Z_RQ + LANE * pp:Z_RQ + LANE * (pp + 1)]
                k = z_ref[rows, Z_RK + LANE * pp:Z_RK + LANE * (pp + 1)] * DK_A ** -0.5
                v = z_ref[rows, Z_RV + 2 * LANE * pp:Z_RV + 2 * LANE * (pp + 1)]
                b = steps_done * jnp.where(lo, _LOG_GAMMA[2 * pp], _LOG_GAMMA[2 * pp + 1])
                z_gate, gain_ref, o_col = Z_RG, gr_ref, 0
            else:
                q = z_ref[rows, Z_GQ + LANE * pp:Z_GQ + LANE * (pp + 1)] * DK_A ** -0.5
                k = z_ref[rows, Z_GK + LANE * pp:Z_GK + LANE * (pp + 1)]
                v = z_ref[rows, Z_GV + 2 * LANE * pp:Z_GV + 2 * LANE * (pp + 1)]
                b = b_gla[:, LANE * pp:LANE * (pp + 1)]
                z_gate, gain_ref, o_col = Z_GG, gg_ref, H_A * DV_A
            b_last = b[C - 1:C, :]
            qd = q * jnp.exp(b)
            k_in = (k * jnp.exp(-b)).astype(BF16)
            k_out = (k * jnp.exp(b_last - b)).astype(BF16)
            vb = v.astype(BF16)
            q2 = jnp.concatenate([jnp.where(lo, qd, 0.0), jnp.where(lo, 0.0, qd)], axis=0).astype(BF16)
            att = lax.dot_general(q2, k_in, _NT, preferred_element_type=F32)
            att = jnp.where(causal2, att, 0.0).astype(BF16)
            st = st_ref[p]
            o = lax.dot_general(qd.astype(BF16), st.astype(BF16), _NT, preferred_element_type=F32)
            o = o + jnp.concatenate([_dot(att[:C], vb[:, :DV_A]), _dot(att[C:], vb[:, DV_A:])], axis=1)
            delta_t = lax.dot_general(vb, k_out, _TN, preferred_element_type=F32)
            st_ref[p] = st * jnp.exp(b_last) + jnp.where(blockdiag, delta_t, 0.0)
            for hh in range(2):
                head = 2 * pp + hh
                oh = _head_rms(o[:, DV_A * hh:DV_A * (hh + 1)])
                gate = z_ref[rows, z_gate + DV_A * head:z_gate + DV_A * (head + 1)]
                gain = gain_ref[:, DV_A * head:DV_A * (head + 1)]
                o_ref[rows, o_col + DV_A * head:o_col + DV_A * (head + 1)] = (oh * gain * _silu(gate)).astype(BF16)

    @pl.when(step == pl.num_programs(1) - 1)
    def _():
        for p in range(4):
            s = st_ref[p].T
            out = sr_ref if p < 2 else sg_ref
            out[2 * (p % 2)] = s[:DK_A, :DV_A]
            out[2 * (p % 2) + 1] = s[DK_A:, DV_A:]


def _even_mixer(z, s_ret, s_gla, wa2, ba, g_ret, g_gla, valid):
    bsz, L, _ = z.shape
    rows = min(L, 4 * CHUNK)
    state = pl.BlockSpec((None, H_A, DK_A, DV_A), lambda b, s: (b, 0, 0, 0))
    st_shape = jax.ShapeDtypeStruct((bsz, H_A, DK_A, DV_A), F32)
    return pl.pallas_call(
        functools.partial(_even_kernel, n_chunks=rows // CHUNK, valid=valid),
        out_shape=[jax.ShapeDtypeStruct((bsz, L, D_MODEL), BF16), st_shape, st_shape],
        grid=(bsz, L // rows),
        in_specs=[pl.BlockSpec((None, rows, Z_W), lambda b, s: (b, s, 0)), state, state,
                  _resident((LANE, 2 * LANE)), _resident((1, 2 * LANE)),
                  _resident((1, H_A * DV_A)), _resident((1, H_A * DV_A))],
        out_specs=[pl.BlockSpec((None, rows, D_MODEL), lambda b, s: (b, s, 0)), state, state],
        scratch_shapes=[pltpu.VMEM((4, 2 * DV_A, 2 * DK_A), F32)],
        compiler_params=_params(("arbitrary", "arbitrary"), 32), name="even_mixer")(
            z, s_ret, s_gla, wa2, ba, g_ret, g_gla)


def _diff_lambda(lq1_ref, lk1_ref, lq2_ref, lk2_ref, lam_init):
    a = jnp.exp(jnp.sum(lq1_ref[...] * lk1_ref[...], axis=-1, keepdims=True))
    b = jnp.exp(jnp.sum(lq2_ref[...] * lk2_ref[...], axis=-1, keepdims=True))
    return a - b + lam_init


def _two_maps(q):
    lane = lax.broadcasted_iota(jnp.int32, q.shape, 1)
    lo = lane < DH_C
    return jnp.concatenate([jnp.where(lo, q, 0.0), jnp.where(lo, 0.0, q)], axis=0)


def _dprompt_kernel(slope_ref, q_ref, k_ref, v_ref, kpos_ref, lq1_ref, lk1_ref, lq2_ref, lk2_ref, gain_ref,
                    o_ref, kaug_ref, vt_ref, s_ref, *, tq, lam_init):
    h = pl.program_id(1)
    nk = vt_ref.shape[0]
    kaug_ref[:, :LANE] = k_ref[...]
    kaug_ref[:, LANE:] = kpos_ref[...]
    for j in range(nk):
        vt_ref[j] = v_ref[j * tq:(j + 1) * tq, :].astype(F32).T.astype(BF16)

    key = lax.broadcasted_iota(jnp.int32, (tq, 2 * tq), 0)
    qry = lax.broadcasted_iota(jnp.int32, (tq, 2 * tq), 1)
    causal = key <= jnp.where(qry >= tq, qry - tq, qry)
    lam = _diff_lambda(lq1_ref, lk1_ref, lq2_ref, lk2_ref, lam_init)

    def score(qb, slot):
        n = qb + 1
        rows = slice(qb * tq, (qb + 1) * tq)
        q2 = _two_maps(q_ref[rows, :].astype(F32) * DH_C ** -0.5)
        lane = lax.broadcasted_iota(jnp.int32, q2.shape, 1)
        q2aug = jnp.concatenate([q2, jnp.where(lane < 2, slope_ref[h], 0.0)], axis=1).astype(BF16)
        s_all = lax.dot_general(kaug_ref[:n * tq, :], q2aug, _NT, preferred_element_type=F32)
        m = None
        for j in range(n):
            s = s_all[j * tq:(j + 1) * tq]
            if j == n - 1:
                s = jnp.where(causal, s, NEG)
            s_ref[slot + j] = s
            mj = jnp.max(s, axis=0, keepdims=True)
            m = mj if m is None else jnp.maximum(m, mj)
        return m

    def apply(qb, slot, m):
        n = qb + 1
        rows = slice(qb * tq, (qb + 1) * tq)
        l = None
        acc = None
        for j in range(n):
            p = jnp.exp(s_ref[slot + j] - m)
            lj = jnp.sum(p, axis=0, keepdims=True)
            aj = _dot(vt_ref[j], p.astype(BF16))
            l = lj if l is None else l + lj
            acc = aj if acc is None else acc + aj
        o = acc / l
        od = (o[:, :tq] - lam * o[:, tq:]).T
        o_ref[rows, :] = (_head_rms(od) * gain_ref[...] * (1.0 - lam_init)).astype(BF16)

    slots = [qb * (qb + 1) // 2 for qb in range(nk)]
    ahead = 1
    maxima = [score(qb, slots[qb]) for qb in range(min(ahead, nk))]
    for qb in range(nk):
        if qb + ahead < nk:
            maxima.append(score(qb + ahead, slots[qb + ahead]))
        apply(qb, slots[qb], maxima[qb])


def _diff_prompt(q, k, v, slopes, lams, gain, lam_init):
    bsz, L, _ = q.shape
    tq = 256
    nq = L // tq
    hd = 2 * DH_C
    pos = np.arange(L)
    kpos = np.zeros((L, LANE), np.float32)
    kpos[:, 0] = pos // 16 * 16
    kpos[:, 1] = pos % 16
    assert L <= 4096 and L % tq == 0
    blk = pl.BlockSpec((None, L, hd), lambda b, h: (b, 0, h))
    vec = lambda n: pl.BlockSpec((1, n), lambda b, h: (0, 0))
    return pl.pallas_call(
        functools.partial(_dprompt_kernel, tq=tq, lam_init=lam_init),
        out_shape=jax.ShapeDtypeStruct((bsz, L, D_MODEL), BF16),
        grid=(bsz, H_C),
        in_specs=[pl.BlockSpec(memory_space=pltpu.SMEM), blk, blk, blk,
                  pl.BlockSpec((L, LANE), lambda b, h: (0, 0)),
                  vec(DH_C), vec(DH_C), vec(DH_C), vec(DH_C), vec(hd)],
        out_specs=blk,
        scratch_shapes=[pltpu.VMEM((L, 2 * LANE), BF16), pltpu.VMEM((nq, hd, tq), BF16),
                        pltpu.VMEM((nq * (nq + 1) // 2, tq, 2 * tq), F32)],
        compiler_params=_params(("arbitrary", "arbitrary"), 40), name="diff_prompt")(
            slopes, q, k, v, jnp.asarray(kpos, BF16), *lams, gain)


PAGES_PER_STEP = 8
RING = 3


def _row_max(s):
    m = s[:, :LANE]
    for j in range(1, s.shape[1] // LANE):
        m = jnp.maximum(m, s[:, j * LANE:(j + 1) * LANE])
    return jnp.broadcast_to(jnp.max(m, axis=1, keepdims=True), m.shape)


def _row_sum(p):
    return jnp.broadcast_to(jnp.sum(p, axis=1, keepdims=True), p.shape)


def _dsample_kernel(pt_ref, q_ref, kn_ref, vn_ref, t1_ref, t1n_ref, slope_ref, lq1_ref, lk1_ref, lq2_ref,
                    lk2_ref, gain_ref, ck_hbm, cv_hbm, o_ref, qall_ref, m_ref, corr_ref, l_ref, acc_ref, sa_ref,
                    sb_ref, p_ref, kbuf, vbuf, ksem, vsem, *, lam_init, n_pages, layer):
    P = PAGES_PER_STEP
    b = pl.program_id(0)
    j = pl.program_id(1)
    n_batch = pl.num_programs(0)
    n_steps = n_pages // P
    T = q_ref.shape[0]
    hd = 2 * DH_C
    n_blk = PAGE * H_C // LANE
    wide = 2 * LANE

    def page_copies(hbm, buf, sem, row, step):
        slot = (row * n_steps + step) % RING
        return [pltpu.make_async_copy(hbm.at[layer, pt_ref[row, step * P + i]], buf.at[slot, i], sem.at[slot])
                for i in range(P)]

    def start_for(row, jj):
        @pl.when(jnp.logical_and(row < n_batch, jj < n_steps))
        def _():
            for c in page_copies(ck_hbm, kbuf, ksem, row, jj):
                c.start()

        @pl.when(jnp.logical_and(row < n_batch, jj >= 1))
        def _():
            for c in page_copies(cv_hbm, vbuf, vsem, row, jj - 1):
                c.start()

    @pl.when(jnp.logical_and(b == 0, j == 0))
    def _():
        start_for(b, j)
        start_for(b, j + 1)

    ahead = b * (n_steps + 1) + j + 2
    start_for(ahead // (n_steps + 1), ahead % (n_steps + 1))

    @pl.when(j == 0)
    def _():
        for h in range(H_C):
            qh = q_ref[:, hd * h:hd * (h + 1)].astype(F32) * DH_C ** -0.5
            qall_ref[2 * T * h:2 * T * (h + 1), :] = _two_maps(qh).astype(BF16)
        m_ref[...] = jnp.full_like(m_ref, NEG)
        corr_ref[...] = jnp.ones_like(corr_ref)
        l_ref[...] = jnp.zeros_like(l_ref)
        acc_ref[...] = jnp.zeros_like(acc_ref)

    qall = qall_ref[...]
    page_off = lambda step, i: slope_ref[...] * ((step * P + i) * PAGE).astype(F32)

    def score(s_w, m_prev):
        for c in page_copies(ck_hbm, kbuf, ksem, b, j):
            c.wait()
        slot = (b * n_steps + j) % RING
        rm = None
        for i in range(P):
            pm = None
            for blk in range(PAGE * H_C // wide):
                keys = slice(blk * wide // H_C, (blk + 1) * wide // H_C)
                k2 = kbuf[slot, i, keys].reshape(wide, hd).astype(BF16)
                s = lax.dot_general(qall, k2, _NT, preferred_element_type=F32) + t1_ref[:, blk * wide:(blk + 1) * wide]
                s_w[i, :, blk * wide:(blk + 1) * wide] = s
                mb = jnp.maximum(s[:, :LANE], s[:, LANE:])
                pm = mb if pm is None else jnp.maximum(pm, mb)
            r = _row_max(pm) + page_off(j, i)
            rm = r if rm is None else jnp.maximum(rm, r)
        m_new = jnp.maximum(m_prev, rm)
        corr_ref[...] = jnp.exp(m_prev - m_new)
        m_ref[...] = m_new

    def apply(s_r, m_prev, corr_prev):
        for c in page_copies(cv_hbm, vbuf, vsem, b, j - 1):
            c.wait()
        slot = (b * n_steps + j - 1) % RING
        lsum = jnp.zeros((LANE, LANE), F32)
        pv = None
        for i in range(P):
            sub = m_prev - page_off(j - 1, i)
            for blk in range(n_blk):
                p = jnp.exp(s_r[i, :, blk * LANE:(blk + 1) * LANE] - sub)
                lsum = lsum + p
                p_ref[i, :, blk * LANE:(blk + 1) * LANE] = p.astype(BF16)
            v2 = vbuf[slot, i].reshape(PAGE * H_C, hd).astype(BF16)
            d = _dot(p_ref[i], v2)
            pv = d if pv is None else pv + d
        l_ref[...] = l_ref[...] * corr_prev + _row_sum(lsum)
        acc_ref[...] = acc_ref[...] * corr_prev + pv

    def both(s_w, s_r):
        m_prev = m_ref[...]
        corr_prev = corr_ref[...]
        score(s_w, m_prev)
        apply(s_r, m_prev, corr_prev)

    middle = jnp.logical_and(j > 0, j < n_steps)
    pl.when(j == 0)(lambda: score(sa_ref, m_ref[...]))
    pl.when(jnp.logical_and(middle, j % 2 == 0))(functools.partial(both, sa_ref, sb_ref))
    pl.when(jnp.logical_and(middle, j % 2 == 1))(functools.partial(both, sb_ref, sa_ref))
    last_scores = sb_ref if n_steps % 2 == 0 else sa_ref
    pl.when(j == n_steps)(lambda: apply(last_scores, m_ref[...], corr_ref[...]))

    @pl.when(j == n_steps)
    def _():
        s = lax.dot_general(qall, kn_ref[...], _NT, preferred_element_type=F32) + t1n_ref[...]
        off = slope_ref[...] * float(n_pages * PAGE)
        m_prev = m_ref[...]
        m_fin = jnp.maximum(m_prev, _row_max(s) + off)
        c = jnp.exp(m_prev - m_fin)
        p = jnp.exp(s - (m_fin - off))
        l = l_ref[...] * c + _row_sum(p)
        o = (acc_ref[...] * c + _dot(p.astype(BF16), vn_ref[...])) / l
        lam = _diff_lambda(lq1_ref, lk1_ref, lq2_ref, lk2_ref, lam_init)
        for h in range(H_C):
            od = o[2 * T * h:2 * T * h + T] - lam * o[2 * T * h + T:2 * T * (h + 1)]
            o_ref[:, hd * h:hd * (h + 1)] = (_head_rms(od) * gain_ref[...] * (1.0 - lam_init)).astype(BF16)


def _diff_sample(q, k_new, v_new, cache_k, cache_v, layer, page_table, lams, gain, lam_init):
    bsz, T, _ = q.shape
    n_pages = page_table.shape[1]
    hd = 2 * DH_C
    P = PAGES_PER_STEP
    n_rows = 2 * T * H_C
    assert n_rows == LANE and n_pages % P == 0
    c = np.arange(n_rows)
    c_h, c_t = c // (2 * T), c % T
    slope_c = 2.0 ** (-8.0 * (c_h + 1) / H_C)
    r = np.arange(PAGE * H_C)
    r_key, r_h = r // H_C, r % H_C
    same_head = r_h[None, :] == c_h[:, None]
    t1 = np.where(same_head, slope_c[:, None] * r_key[None, :], NEG)
    new_ok = same_head[:, :LANE] & (r_key[None, :LANE] <= c_t[:, None]) & (r_key[None, :LANE] < T)
    t1n = np.where(new_ok, slope_c[:, None] * r_key[None, :LANE], NEG)
    slope_tile = np.broadcast_to(slope_c[:, None], (n_rows, LANE))
    pad = lambda a: jnp.pad(a.reshape(bsz, T * H_C, hd), ((0, 0), (0, LANE - T * H_C), (0, 0)))

    n_steps = n_pages // P
    const = lambda shape: pl.BlockSpec(shape, lambda b, j, pt: (0,) * len(shape))
    per_b = lambda shape: pl.BlockSpec((None,) + shape, lambda b, j, pt: (b,) + (0,) * len(shape))
    scores = (P, n_rows, PAGE * H_C)
    stat = pltpu.VMEM((n_rows, LANE), F32)
    grid_spec = pltpu.PrefetchScalarGridSpec(
        num_scalar_prefetch=1, grid=(bsz, n_steps + 1),
        in_specs=[per_b((T, D_MODEL)), per_b((LANE, hd)), per_b((LANE, hd)),
                  const((n_rows, PAGE * H_C)), const((n_rows, LANE)), const((n_rows, LANE)),
                  const((1, DH_C)), const((1, DH_C)), const((1, DH_C)), const((1, DH_C)), const((1, hd))]
                 + [pl.BlockSpec(memory_space=pl.ANY)] * 2,
        out_specs=per_b((T, D_MODEL)),
        scratch_shapes=[pltpu.VMEM((n_rows, hd), BF16), stat, stat, stat, pltpu.VMEM((n_rows, hd), F32),
                        pltpu.VMEM(scores, F32), pltpu.VMEM(scores, F32), pltpu.VMEM(scores, BF16),
                        pltpu.VMEM((RING, P, PAGE, H_C, hd), F32), pltpu.VMEM((RING, P, PAGE, H_C, hd), F32),
                        pltpu.SemaphoreType.DMA((RING,)), pltpu.SemaphoreType.DMA((RING,))])
    return pl.pallas_call(
        functools.partial(_dsample_kernel, lam_init=lam_init, n_pages=n_pages, layer=layer),
        out_shape=jax.ShapeDtypeStruct((bsz, T, D_MODEL), BF16), grid_spec=grid_spec,
        compiler_params=_params(("arbitrary", "arbitrary"), 48), name="diff_sample")(
            page_table, q, pad(k_new), pad(v_new), jnp.asarray(t1, F32), jnp.asarray(t1n, F32),
            jnp.asarray(slope_tile, F32), *lams, gain, cache_k, cache_v)


def _cross_kernel(x_ref, g_ref, wq_ref, mk_ref, mv_ref, wo_ref, o_ref):
    G, T, _ = x_ref.shape
    x = x_ref[...].reshape(G * T, D_MODEL)
    q = _dot(_rms(x, g_ref[...]).astype(BF16), wq_ref[...])
    rows = []
    for g in range(G):
        heads = []
        for h in range(H_X):
            cols = slice(DH_X * h, DH_X * (h + 1))
            qh = q[g * T:(g + 1) * T, cols].astype(BF16)
            s = lax.dot_general(qh, mk_ref[g, :, cols].astype(BF16), _NT, preferred_element_type=F32) * DH_X ** -0.5
            p = jnp.exp(s - jnp.max(s, axis=-1, keepdims=True))
            p = p / jnp.sum(p, axis=-1, keepdims=True)
            heads.append(_dot(p.astype(BF16), mv_ref[g, :, cols].astype(BF16)))
        rows.append(jnp.concatenate(heads, axis=1))
    o = jnp.concatenate(rows, axis=0).astype(BF16)
    o_ref[...] = (x + _dot(o, wo_ref[...])).reshape(G, T, D_MODEL)


def _cross(x, gains, wq, mk, mv, wo, li, group, rows):
    bsz, L, _ = x.shape
    xs = pl.BlockSpec((group, rows, D_MODEL), lambda b, t: (b, t, 0))
    mem = pl.BlockSpec((None, group, N_MEM, D_MODEL), lambda b, t: (li, b, 0, 0))
    return pl.pallas_call(
        _cross_kernel, out_shape=jax.ShapeDtypeStruct(x.shape, F32), grid=(bsz // group, L // rows),
        in_specs=[xs, _layer(gains, li), _layer(wq, li), mem, mem, _layer(wo, li)],
        out_specs=xs, compiler_params=_params(("arbitrary", "arbitrary"), 48), name="cross")(
            x, gains, wq, mk, mv, wo)


def _cross_cache_kernel(x_ref, g_ref, wq_ref, mk_ref, mv_ref, wo_ref, o_ref):
    G, T, _ = x_ref.shape
    n_kv = N_MEM * H_X
    x = x_ref[...].reshape(G * T, D_MODEL)
    q = _dot(_rms(x, g_ref[...]).astype(BF16), wq_ref[...])
    same = (lax.broadcasted_iota(jnp.int32, (H_X * T, n_kv), 0) // T
            == lax.broadcasted_iota(jnp.int32, (H_X * T, n_kv), 1) % H_X)
    rows = []
    for g in range(G):
        qg = q[g * T:(g + 1) * T]
        q4 = jnp.concatenate([qg[:, DH_X * h:DH_X * (h + 1)] for h in range(H_X)], axis=0).astype(BF16)
        k2 = mk_ref[g].reshape(n_kv, DH_X).astype(BF16)
        v2 = mv_ref[g].reshape(n_kv, DH_X).astype(BF16)
        s = lax.dot_general(q4, k2, _NT, preferred_element_type=F32) * DH_X ** -0.5
        s = jnp.where(same, s, NEG)
        p = jnp.exp(s - jnp.max(s, axis=-1, keepdims=True))
        p = p / jnp.sum(p, axis=-1, keepdims=True)
        o4 = _dot(p.astype(BF16), v2)
        rows.append(jnp.concatenate([o4[T * h:T * (h + 1)] for h in range(H_X)], axis=1))
    o = jnp.concatenate(rows, axis=0).astype(BF16)
    o_ref[...] = (x + _dot(o, wo_ref[...])).reshape(G, T, D_MODEL)


def _cross_cache(x, gains, wq, cache_k, cache_v, wo, li, group):
    bsz, T, _ = x.shape
    xs = pl.BlockSpec((group, T, D_MODEL), lambda b: (b, 0, 0))
    mem = pl.BlockSpec((None, group, N_MEM, H_X, DH_X), lambda b: (li, b, 0, 0, 0))
    return pl.pallas_call(
        _cross_cache_kernel, out_shape=jax.ShapeDtypeStruct(x.shape, F32), grid=(bsz // group,),
        in_specs=[xs, _layer(gains, li), _layer(wq, li), mem, mem, _layer(wo, li)],
        out_specs=xs, compiler_params=_params(("arbitrary",), 56), name="cross_cache")(
            x, gains, wq, cache_k, cache_v, wo)


def kernel(x_prompt, x_sample, state_ret, state_gla, cache_k, cache_v, cache_mem_k, cache_mem_v, page_table, mem_prompt, norm_ffn1, ffn1_wg, ffn1_wu, ffn1_wd, norm_mix, w_in_even, gla_wa2, gla_ba, ret_gain, gla_gain, w_in_odd, lam_q1, lam_k1, lam_q2, lam_k2, diff_gain, w_mix_out, norm_x, x_wq, x_wk, x_wv, x_wo, norm_ffn2, ffn2_wg, ffn2_wu, ffn2_wd, final_norm):
    bp, lp, _ = x_prompt.shape
    bs, ls, _ = x_sample.shape
    depth = norm_ffn1.shape[0]
    bf = lambda w: w.astype(BF16)
    xp = x_prompt.reshape(bp * lp, D_MODEL)
    xs = x_sample.reshape(bs * ls, D_MODEL)
    slopes = jnp.asarray([2.0 ** (-8.0 * (h + 1) / H_C) for h in range(H_C)], F32)
    ret_p, ret_s, gla_p, gla_s, kr_p, vr_p, kr_s, vr_s = ([] for _ in range(8))
    rows = lambda g: g.reshape(g.shape[0], 1, g.shape[1])
    g_ffn1, g_mix, g_x, g_ffn2 = rows(norm_ffn1), rows(norm_mix), rows(norm_x), rows(norm_ffn2)
    w1 = (bf(ffn1_wg), bf(ffn1_wu), bf(ffn1_wd))
    w2 = (bf(ffn2_wg), bf(ffn2_wu), bf(ffn2_wd))
    w_out, wq, wo = bf(w_mix_out), bf(x_wq), bf(x_wo)
    w_even = bf(jnp.pad(w_in_even, ((0, 0), (0, 0), (0, Z_W - w_in_even.shape[2]))))
    w_odd = bf(w_in_odd)
    w_kv = bf(jnp.concatenate([w for li in range(depth) for w in (x_wk[li], x_wv[li])], axis=1))
    mk, mv, mk4, mv4 = _mem_proj(mem_prompt.reshape(bp * N_MEM, D_MODEL), w_kv, depth)
    mk = mk.reshape(depth, bp, N_MEM, D_MODEL)
    mv = mv.reshape(depth, bp, N_MEM, D_MODEL)

    for li in range(depth):
        xp = _ffn(xp, g_ffn1, *w1, li)
        xs = _ffn(xs, g_ffn1, *w1, li)
        if li % 2 == 0:
            e = li // 2
            wa2 = bf(jnp.pad(gla_wa2[e], ((0, LANE - GLA_RANK), (0, 0))))
            prm = (wa2, gla_ba[e].reshape(1, -1), ret_gain[e].reshape(1, -1), gla_gain[e].reshape(1, -1))
            (zp,) = _proj(xp, g_mix, li, w_even, e, ((0, Z_W, F32),))
            (zs,) = _proj(xs, g_mix, li, w_even, e, ((0, Z_W, F32),))
            zero = jnp.zeros((bp, H_A, DK_A, DV_A), F32)
            op, sr, sg = _even_mixer(zp.reshape(bp, lp, Z_W), zero, zero, *prm, valid=CHUNK)
            ret_p.append(sr); gla_p.append(sg)
            zs = jnp.pad(zs.reshape(bs, ls, Z_W), ((0, 0), (0, CHUNK - ls), (0, 0)))
            os_, sr, sg = _even_mixer(zs, state_ret[e], state_gla[e], *prm, valid=ls)
            ret_s.append(sr); gla_s.append(sg)
            os_ = os_[:, :ls]
        else:
            o = li // 2
            lam_init = 0.8 - 0.6 * math.exp(-0.3 * li)
            lams = tuple(a[o].reshape(1, DH_C) for a in (lam_q1, lam_k1, lam_q2, lam_k2))
            gain = diff_gain[o].reshape(1, 2 * DH_C)
            hd = 2 * DH_C
            outs = ((0, D_MODEL, BF16), (D_MODEL, D_MODEL, BF16), (2 * D_MODEL, D_MODEL, BF16),
                    (D_MODEL, D_MODEL, hd), (2 * D_MODEL, D_MODEL, hd))
            qp, kp, vp, kp4, vp4 = _proj(xp, g_mix, li, w_odd, o, outs)
            qs, ks, vs, ks4, vs4 = _proj(xs, g_mix, li, w_odd, o, outs)
            shp = lambda a: a.reshape(bp, lp, D_MODEL)
            shs = lambda a: a.reshape(bs, ls, D_MODEL)
            op = _diff_prompt(shp(qp), shp(kp), shp(vp), slopes, lams, gain, lam_init)
            os_ = _diff_sample(shs(qs), shs(ks), shs(vs), cache_k, cache_v, o, page_table, lams, gain, lam_init)
            kr_p.append(kp4.reshape(bp, lp, H_C, hd)); vr_p.append(vp4.reshape(bp, lp, H_C, hd))
            kr_s.append(ks4.reshape(bs, ls, H_C, hd)); vr_s.append(vs4.reshape(bs, ls, H_C, hd))
        xp = _outproj(op.reshape(bp * lp, D_MODEL), w_out, li, xp)
        xs = _outproj(os_.reshape(bs * ls, D_MODEL), w_out, li, xs)
        xp = _cross(xp.reshape(bp, lp, D_MODEL), g_x, wq, mk, mv, wo, li, group=1, rows=512).reshape(bp * lp, D_MODEL)
        xs = _cross_cache(xs.reshape(bs, ls, D_MODEL), g_x, wq, cache_mem_k, cache_mem_v, wo, li,
                          group=4).reshape(bs * ls, D_MODEL)
        fg = final_norm if li == depth - 1 else None
        xp = _ffn(xp, g_ffn2, *w2, li, final_gain=fg)
        xs = _ffn(xs, g_ffn2, *w2, li, final_gain=fg)

    return (xp.reshape(bp, lp, D_MODEL), xs.reshape(bs, ls, D_MODEL),
            jnp.stack(ret_p), jnp.stack(ret_s), jnp.stack(gla_p), jnp.stack(gla_s),
            jnp.stack(kr_p), jnp.stack(vr_p), jnp.stack(kr_s), jnp.stack(vr_s),
            mk4.reshape(depth, bp, N_MEM, H_X, DH_X), mv4.reshape(depth, bp, N_MEM, H_X, DH_X))
```

```python
import functools
import math

import numpy as np
import jax
import jax.numpy as jnp
from jax import lax
from jax.experimental import pallas as pl
from jax.experimental.pallas import tpu as pltpu

F32 = jnp.float32
BF16 = jnp.bfloat16

D_MODEL = 1024
D_FF = 2816
EPS = 1e-6
CHUNK = 64
H_A, DK_A, DV_A = 4, 64, 128
GLA_RANK = 16
GLA_TAU = 16.0
H_C, DH_C = 8, 64
H_X, DH_X = 4, 256
N_MEM = 256
PAGE = 128
NEG = -1e30

LANE = 128
FF_CHUNK = 256
MIB = 1024 * 1024

Z_RQ, Z_RK, Z_RV, Z_RG, Z_GQ, Z_GK, Z_GV, Z_GG, Z_GLR = 0, 256, 512, 1024, 1536, 1792, 2048, 2560, 3072
Z_W = 3200

_NT = (((1,), (1,)), ((), ()))
_TN = (((0,), (0,)), ((), ()))


def _params(sem, vmem_mib):
    return pltpu.CompilerParams(dimension_semantics=sem, vmem_limit_bytes=vmem_mib * MIB)


def _resident(shape):
    nd = len(shape)
    return pl.BlockSpec(shape, lambda *_: (0,) * nd, pipeline_mode=pl.Buffered(1))


def _layer(stacked, li):
    shape = stacked.shape[1:]
    return pl.BlockSpec((None,) + shape, lambda *_: (li,) + (0,) * len(shape), pipeline_mode=pl.Buffered(1))


def _rms(x, g):
    return x * lax.rsqrt(jnp.mean(x * x, axis=-1, keepdims=True) + EPS) * g


def _head_rms(o):
    return o * lax.rsqrt(jnp.mean(o * o, axis=-1, keepdims=True) + EPS)


def _silu(x):
    return x * jax.nn.sigmoid(x)


def _dot(a, b):
    return jnp.dot(a, b, preferred_element_type=F32)


def _ffn_kernel(x_ref, g_ref, wg_ref, wu_ref, wd_ref, *rest, final):
    if final:
        fg_ref, o_ref, a_ref = rest
    else:
        o_ref, a_ref = rest
    x = x_ref[...]
    h = _rms(x, g_ref[...]).astype(BF16)
    for c in range(D_FF // FF_CHUNK):
        sl = slice(c * FF_CHUNK, (c + 1) * FF_CHUNK)
        g = _dot(h, wg_ref[:, sl])
        u = _dot(h, wu_ref[:, sl])
        a_ref[:, sl] = (_silu(g) * u).astype(BF16)
    y = x + 0.5 * _dot(a_ref[...], wd_ref[...])
    if final:
        y = _rms(y, fg_ref[...])
    o_ref[...] = y


def _ffn(x, gains, wg, wu, wd, li, final_gain=None):
    n = x.shape[0]
    tm = min(1024, n)
    final = final_gain is not None
    row = pl.BlockSpec((tm, D_MODEL), lambda i: (i, 0))
    in_specs = [row] + [_layer(a, li) for a in (gains, wg, wu, wd)]
    args = [x, gains, wg, wu, wd]
    if final:
        in_specs.append(_resident((1, D_MODEL)))
        args.append(final_gain.reshape(1, D_MODEL))
    return pl.pallas_call(
        functools.partial(_ffn_kernel, final=final),
        out_shape=jax.ShapeDtypeStruct((n, D_MODEL), F32),
        grid=(n // tm,), in_specs=in_specs, out_specs=row,
        scratch_shapes=[pltpu.VMEM((tm, D_FF), BF16)],
        compiler_params=_params(("arbitrary",), 56), name="ffn")(*args)


def _heads_store(o_ref, y, width):
    for hh in range(y.shape[1] // width):
        o_ref[:, hh, :] = y[:, hh * width:(hh + 1) * width]


def _proj_kernel(x_ref, g_ref, w_ref, *o_refs, outs):
    h = _rms(x_ref[...], g_ref[...]).astype(BF16)
    done = {}
    for o_ref, (off, n, kind) in zip(o_refs, outs):
        if (off, n) not in done:
            done[(off, n)] = _dot(h, w_ref[:, off:off + n])
        y = done[(off, n)]
        if isinstance(kind, int):
            _heads_store(o_ref, y, kind)
        else:
            o_ref[...] = y.astype(o_ref.dtype)


def _proj(x, gains, li, w, wi, outs):
    n, k = x.shape
    tm = min(512, n)
    in_specs = [pl.BlockSpec((tm, k), lambda i: (i, 0)), _layer(gains, li), _layer(w, wi)]
    args = [x, gains, w]
    out_shape, out_specs = [], []
    for _, width, kind in outs:
        if isinstance(kind, int):
            out_shape.append(jax.ShapeDtypeStruct((n, width // kind, kind), F32))
            out_specs.append(pl.BlockSpec((tm, width // kind, kind), lambda i: (i, 0, 0)))
        else:
            out_shape.append(jax.ShapeDtypeStruct((n, width), kind))
            out_specs.append(pl.BlockSpec((tm, width), lambda i: (i, 0)))
    return pl.pallas_call(
        functools.partial(_proj_kernel, outs=outs),
        out_shape=out_shape, grid=(n // tm,), in_specs=in_specs, out_specs=out_specs,
        compiler_params=_params(("arbitrary",), 40), name="proj")(*args)


def _mem_kernel(x_ref, w_ref, mk_ref, mv_ref, mk4_ref, mv4_ref):
    h = x_ref[...].astype(BF16)
    for l in range(mk_ref.shape[0]):
        for t, (o2, o4) in enumerate(((mk_ref, mk4_ref), (mv_ref, mv4_ref))):
            y = _dot(h, w_ref[:, (2 * l + t) * D_MODEL:(2 * l + t + 1) * D_MODEL])
            o2[l] = y.astype(BF16)
            _heads_store(o4.at[l], y, DH_X)


def _mem_proj(mem, w, depth):
    n = mem.shape[0]
    tm = min(512, n)
    flat = pl.BlockSpec((depth, tm, D_MODEL), lambda i: (0, i, 0))
    heads = pl.BlockSpec((depth, tm, H_X, DH_X), lambda i: (0, i, 0, 0))
    return pl.pallas_call(
        _mem_kernel,
        out_shape=[jax.ShapeDtypeStruct((depth, n, D_MODEL), BF16)] * 2
        + [jax.ShapeDtypeStruct((depth, n, H_X, DH_X), F32)] * 2,
        grid=(n // tm,), in_specs=[pl.BlockSpec((tm, D_MODEL), lambda i: (i, 0)), _resident(w.shape)],
        out_specs=[flat, flat, heads, heads],
        compiler_params=_params(("arbitrary",), 48), name="mem_proj")(mem, w)


def _outproj_kernel(a_ref, w_ref, r_ref, o_ref):
    o_ref[...] = r_ref[...] + _dot(a_ref[...], w_ref[...])


def _outproj(a, w, li, res):
    n, k = a.shape
    tm = min(512, n)
    return pl.pallas_call(
        _outproj_kernel, out_shape=jax.ShapeDtypeStruct((n, D_MODEL), F32), grid=(n // tm,),
        in_specs=[pl.BlockSpec((tm, k), lambda i: (i, 0)), _layer(w, li),
                  pl.BlockSpec((tm, D_MODEL), lambda i: (i, 0))],
        out_specs=pl.BlockSpec((tm, D_MODEL), lambda i: (i, 0)),
        compiler_params=_params(("arbitrary",), 32), name="outproj")(a, w, res)


_LOG_GAMMA = [math.log1p(-2.0 ** (-5.0 - h)) for h in range(H_A)]


def _even_kernel(z_ref, sr0_ref, sg0_ref, wa2_ref, ba_ref, gr_ref, gg_ref, o_ref, sr_ref, sg_ref, st_ref,
                 *, n_chunks, valid):
    C = CHUNK
    step = pl.program_id(1)

    def pair_state(a, b):
        zero = jnp.zeros((DK_A, DV_A), F32)
        s = jnp.concatenate([jnp.concatenate([a, zero], axis=1), jnp.concatenate([zero, b], axis=1)], axis=0)
        return s.T

    @pl.when(step == 0)
    def _():
        for pp in range(2):
            st_ref[pp] = pair_state(sr0_ref[2 * pp], sr0_ref[2 * pp + 1])
            st_ref[2 + pp] = pair_state(sg0_ref[2 * pp], sg0_ref[2 * pp + 1])

    row = lax.broadcasted_iota(jnp.int32, (C, LANE), 0)
    lane = lax.broadcasted_iota(jnp.int32, (C, LANE), 1)
    lo = lane < DK_A
    r2 = lax.broadcasted_iota(jnp.int32, (2 * C, C), 0)
    c2 = lax.broadcasted_iota(jnp.int32, (2 * C, C), 1)
    causal2 = jnp.where(r2 >= C, r2 - C, r2) >= c2
    tr = lax.broadcasted_iota(jnp.int32, (C, C), 0)
    tc = lax.broadcasted_iota(jnp.int32, (C, C), 1)
    tril = (tr >= tc).astype(F32)
    sr_ = lax.broadcasted_iota(jnp.int32, (2 * DV_A, 2 * DK_A), 0)
    sc_ = lax.broadcasted_iota(jnp.int32, (2 * DV_A, 2 * DK_A), 1)
    blockdiag = (sr_ >= DV_A) == (sc_ >= DK_A)
    steps_done = jnp.minimum(row + 1, valid).astype(F32)

    for ci in range(n_chunks):
        rows = slice(ci * C, (ci + 1) * C)
        glr = z_ref[rows, Z_GLR:Z_GLR + LANE].astype(BF16)
        xg = _dot(glr, wa2_ref[...]) + ba_ref[...]
        la = (jnp.minimum(xg, 0.0) - jnp.log1p(jnp.exp(-jnp.abs(xg)))) / GLA_TAU
        if valid < C:
            la = jnp.where(lax.broadcasted_iota(jnp.int32, la.shape, 0) < valid, la, 0.0)
        b_gla = jnp.dot(tril, la, precision=lax.Precision.HIGHEST, preferred_element_type=F32)
        for p in range(4):
            pp = p % 2
            if p < 2:
                q = z_ref[rows, Z_RQ + LANE * pp:Z_RQ + LANE * (pp + 1)]
                k = z_ref[rows, Z_RK + LANE * pp:Z_RK + LANE * (pp + 1)] * DK_A ** -0.5
                v = z_ref[rows, Z_RV + 2 * LANE * pp:Z_RV + 2 * LANE * (pp + 1)]
                b = steps_done * jnp.where(lo, _LOG_GAMMA[2 * pp], _LOG_GAMMA[2 * pp + 1])
                z_gate, gain_ref, o_col = Z_RG, gr_ref, 0
            else:
                q = z_ref[rows, Z_GQ + LANE * pp:Z_GQ + LANE * (pp + 1)] * DK_A ** -0.5
                k = z_ref[rows, Z_GK + LANE * pp:Z_GK + LANE * (pp + 1)]
                v = z_ref[rows, Z_GV + 2 * LANE * pp:Z_GV + 2 * LANE * (pp + 1)]
                b = b_gla[:, LANE * pp:LANE * (pp + 1)]
                z_gate, gain_ref, o_col = Z_GG, gg_ref, H_A * DV_A
            b_last = b[C - 1:C, :]
            qd = q * jnp.exp(b)
            k_in = (k * jnp.exp(-b)).astype(BF16)
            k_out = (k * jnp.exp(b_last - b)).astype(BF16)
            vb = v.astype(BF16)
            q2 = jnp.concatenate([jnp.where(lo, qd, 0.0), jnp.where(lo, 0.0, qd)], axis=0).astype(BF16)
            att = lax.dot_general(q2, k_in, _NT, preferred_element_type=F32)
            att = jnp.where(causal2, att, 0.0).astype(BF16)
            st = st_ref[p]
            o = lax.dot_general(qd.astype(BF16), st.astype(BF16), _NT, preferred_element_type=F32)
            o = o + jnp.concatenate([_dot(att[:C], vb[:, :DV_A]), _dot(att[C:], vb[:, DV_A:])], axis=1)
            delta_t = lax.dot_general(vb, k_out, _TN, preferred_element_type=F32)
            st_ref[p] = st * jnp.exp(b_last) + jnp.where(blockdiag, delta_t, 0.0)
            for hh in range(2):
                head = 2 * pp + hh
                oh = _head_rms(o[:, DV_A * hh:DV_A * (hh + 1)])
                gate = z_ref[rows, z_gate + DV_A * head:z_gate + DV_A * (head + 1)]
                gain = gain_ref[:, DV_A * head:DV_A * (head + 1)]
                o_ref[rows, o_col + DV_A * head:o_col + DV_A * (head + 1)] = (oh * gain * _silu(gate)).astype(BF16)

    @pl.when(step == pl.num_programs(1) - 1)
    def _():
        for p in range(4):
            s = st_ref[p].T
            out = sr_ref if p < 2 else sg_ref
            out[2 * (p % 2)] = s[:DK_A, :DV_A]
            out[2 * (p % 2) + 1] = s[DK_A:, DV_A:]


def _even_mixer(z, s_ret, s_gla, wa2, ba, g_ret, g_gla, valid):
    bsz, L, _ = z.shape
    rows = min(L, 4 * CHUNK)
    state = pl.BlockSpec((None, H_A, DK_A, DV_A), lambda b, s: (b, 0, 0, 0))
    st_shape = jax.ShapeDtypeStruct((bsz, H_A, DK_A, DV_A), F32)
    return pl.pallas_call(
        functools.partial(_even_kernel, n_chunks=rows // CHUNK, valid=valid),
        out_shape=[jax.ShapeDtypeStruct((bsz, L, D_MODEL), BF16), st_shape, st_shape],
        grid=(bsz, L // rows),
        in_specs=[pl.BlockSpec((None, rows, Z_W), lambda b, s: (b, s, 0)), state, state,
                  _resident((LANE, 2 * LANE)), _resident((1, 2 * LANE)),
                  _resident((1, H_A * DV_A)), _resident((1, H_A * DV_A))],
        out_specs=[pl.BlockSpec((None, rows, D_MODEL), lambda b, s: (b, s, 0)), state, state],
        scratch_shapes=[pltpu.VMEM((4, 2 * DV_A, 2 * DK_A), F32)],
        compiler_params=_params(("arbitrary", "arbitrary"), 32), name="even_mixer")(
            z, s_ret, s_gla, wa2, ba, g_ret, g_gla)


def _diff_lambda(lq1_ref, lk1_ref, lq2_ref, lk2_ref, lam_init):
    a = jnp.exp(jnp.sum(lq1_ref[...] * lk1_ref[...], axis=-1, keepdims=True))
    b = jnp.exp(jnp.sum(lq2_ref[...] * lk2_ref[...], axis=-1, keepdims=True))
    return a - b + lam_init


def _two_maps(q):
    lane = lax.broadcasted_iota(jnp.int32, q.shape, 1)
    lo = lane < DH_C
    return jnp.concatenate([jnp.where(lo, q, 0.0), jnp.where(lo, 0.0, q)], axis=0)


def _dprompt_kernel(slope_ref, q_ref, k_ref, v_ref, kpos_ref, lq1_ref, lk1_ref, lq2_ref, lk2_ref, gain_ref,
                    o_ref, kaug_ref, vt_ref, s_ref, *, tq, lam_init):
    h = pl.program_id(1)
    nk = vt_ref.shape[0]
    kaug_ref[:, :LANE] = k_ref[...]
    kaug_ref[:, LANE:] = kpos_ref[...]
    for j in range(nk):
        vt_ref[j] = v_ref[j * tq:(j + 1) * tq, :].astype(F32).T.astype(BF16)

    key = lax.broadcasted_iota(jnp.int32, (tq, 2 * tq), 0)
    qry = lax.broadcasted_iota(jnp.int32, (tq, 2 * tq), 1)
    causal = key <= jnp.where(qry >= tq, qry - tq, qry)
    lam = _diff_lambda(lq1_ref, lk1_ref, lq2_ref, lk2_ref, lam_init)

    def score(qb, slot):
        n = qb + 1
        rows = slice(qb * tq, (qb + 1) * tq)
        q2 = _two_maps(q_ref[rows, :].astype(F32) * DH_C ** -0.5)
        lane = lax.broadcasted_iota(jnp.int32, q2.shape, 1)
        q2aug = jnp.concatenate([q2, jnp.where(lane < 2, slope_ref[h], 0.0)], axis=1).astype(BF16)
        s_all = lax.dot_general(kaug_ref[:n * tq, :], q2aug, _NT, preferred_element_type=F32)
        m = None
        for j in range(n):
            s = s_all[j * tq:(j + 1) * tq]
            if j == n - 1:
                s = jnp.where(causal, s, NEG)
            s_ref[slot + j] = s
            mj = jnp.max(s, axis=0, keepdims=True)
            m = mj if m is None else jnp.maximum(m, mj)
        return m

    def apply(qb, slot, m):
        n = qb + 1
        rows = slice(qb * tq, (qb + 1) * tq)
        l = None
        acc = None
        for j in range(n):
            p = jnp.exp(s_ref[slot + j] - m)
            lj = jnp.sum(p, axis=0, keepdims=True)
            aj = _dot(vt_ref[j], p.astype(BF16))
            l = lj if l is None else l + lj
            acc = aj if acc is None else acc + aj
        o = acc / l
        od = (o[:, :tq] - lam * o[:, tq:]).T
        o_ref[rows, :] = (_head_rms(od) * gain_ref[...] * (1.0 - lam_init)).astype(BF16)

    slots = [qb * (qb + 1) // 2 for qb in range(nk)]
    ahead = 1
    maxima = [score(qb, slots[qb]) for qb in range(min(ahead, nk))]
    for qb in range(nk):
        if qb + ahead < nk:
            maxima.append(score(qb + ahead, slots[qb + ahead]))
        apply(qb, slots[qb], maxima[qb])


def _diff_prompt(q, k, v, slopes, lams, gain, lam_init):
    bsz, L, _ = q.shape
    tq = 256
    nq = L // tq
    hd = 2 * DH_C
    pos = np.arange(L)
    kpos = np.zeros((L, LANE), np.float32)
    kpos[:, 0] = pos // 16 * 16
    kpos[:, 1] = pos % 16
    assert L <= 4096 and L % tq == 0
    blk = pl.BlockSpec((None, L, hd), lambda b, h: (b, 0, h))
    vec = lambda n: pl.BlockSpec((1, n), lambda b, h: (0, 0))
    return pl.pallas_call(
        functools.partial(_dprompt_kernel, tq=tq, lam_init=lam_init),
        out_shape=jax.ShapeDtypeStruct((bsz, L, D_MODEL), BF16),
        grid=(bsz, H_C),
        in_specs=[pl.BlockSpec(memory_space=pltpu.SMEM), blk, blk, blk,
                  pl.BlockSpec((L, LANE), lambda b, h: (0, 0)),
                  vec(DH_C), vec(DH_C), vec(DH_C), vec(DH_C), vec(hd)],
        out_specs=blk,
        scratch_shapes=[pltpu.VMEM((L, 2 * LANE), BF16), pltpu.VMEM((nq, hd, tq), BF16),
                        pltpu.VMEM((nq * (nq + 1) // 2, tq, 2 * tq), F32)],
        compiler_params=_params(("arbitrary", "arbitrary"), 40), name="diff_prompt")(
            slopes, q, k, v, jnp.asarray(kpos, BF16), *lams, gain)


PAGES_PER_STEP = 8
RING = 3


def _row_max(s):
    m = s[:, :LANE]
    for j in range(1, s.shape[1] // LANE):
        m = jnp.maximum(m, s[:, j * LANE:(j + 1) * LANE])
    return jnp.broadcast_to(jnp.max(m, axis=1, keepdims=True), m.shape)


def _row_sum(p):
    return jnp.broadcast_to(jnp.sum(p, axis=1, keepdims=True), p.shape)


def _dsample_kernel(pt_ref, q_ref, kn_ref, vn_ref, t1_ref, t1n_ref, slope_ref, lq1_ref, lk1_ref, lq2_ref,
                    lk2_ref, gain_ref, ck_hbm, cv_hbm, o_ref, qall_ref, m_ref, corr_ref, l_ref, acc_ref, sa_ref,
                    sb_ref, p_ref, kbuf, vbuf, ksem, vsem, *, lam_init, n_pages, layer):
    P = PAGES_PER_STEP
    b = pl.program_id(0)
    j = pl.program_id(1)
    n_batch = pl.num_programs(0)
    n_steps = n_pages // P
    T = q_ref.shape[0]
    hd = 2 * DH_C
    n_blk = PAGE * H_C // LANE
    wide = 2 * LANE

    def page_copies(hbm, buf, sem, row, step):
        slot = (row * n_steps + step) % RING
        return [pltpu.make_async_copy(hbm.at[layer, pt_ref[row, step * P + i]], buf.at[slot, i], sem.at[slot])
                for i in range(P)]

    def start_for(row, jj):
        @pl.when(jnp.logical_and(row < n_batch, jj < n_steps))
        def _():
            for c in page_copies(ck_hbm, kbuf, ksem, row, jj):
                c.start()

        @pl.when(jnp.logical_and(row < n_batch, jj >= 1))
        def _():
            for c in page_copies(cv_hbm, vbuf, vsem, row, jj - 1):
                c.start()

    @pl.when(jnp.logical_and(b == 0, j == 0))
    def _():
        start_for(b, j)
        start_for(b, j + 1)

    ahead = b * (n_steps + 1) + j + 2
    start_for(ahead // (n_steps + 1), ahead % (n_steps + 1))

    @pl.when(j == 0)
    def _():
        for h in range(H_C):
            qh = q_ref[:, hd * h:hd * (h + 1)].astype(F32) * DH_C ** -0.5
            qall_ref[2 * T * h:2 * T * (h + 1), :] = _two_maps(qh).astype(BF16)
        m_ref[...] = jnp.full_like(m_ref, NEG)
        corr_ref[...] = jnp.ones_like(corr_ref)
        l_ref[...] = jnp.zeros_like(l_ref)
        acc_ref[...] = jnp.zeros_like(acc_ref)

    qall = qall_ref[...]
    page_off = lambda step, i: slope_ref[...] * ((step * P + i) * PAGE).astype(F32)

    def score(s_w, m_prev):
        for c in page_copies(ck_hbm, kbuf, ksem, b, j):
            c.wait()
        slot = (b * n_steps + j) % RING
        rm = None
        for i in range(P):
            pm = None
            for blk in range(PAGE * H_C // wide):
                keys = slice(blk * wide // H_C, (blk + 1) * wide // H_C)
                k2 = kbuf[slot, i, keys].reshape(wide, hd).astype(BF16)
                s = lax.dot_general(qall, k2, _NT, preferred_element_type=F32) + t1_ref[:, blk * wide:(blk + 1) * wide]
                s_w[i, :, blk * wide:(blk + 1) * wide] = s
                mb = jnp.maximum(s[:, :LANE], s[:, LANE:])
                pm = mb if pm is None else jnp.maximum(pm, mb)
            r = _row_max(pm) + page_off(j, i)
            rm = r if rm is None else jnp.maximum(rm, r)
        m_new = jnp.maximum(m_prev, rm)
        corr_ref[...] = jnp.exp(m_prev - m_new)
        m_ref[...] = m_new

    def apply(s_r, m_prev, corr_prev):
        for c in page_copies(cv_hbm, vbuf, vsem, b, j - 1):
            c.wait()
        slot = (b * n_steps + j - 1) % RING
        lsum = jnp.zeros((LANE, LANE), F32)
        pv = None
        for i in range(P):
            sub = m_prev - page_off(j - 1, i)
            for blk in range(n_blk):
                p = jnp.exp(s_r[i, :, blk * LANE:(blk + 1) * LANE] - sub)
                lsum = lsum + p
                p_ref[i, :, blk * LANE:(blk + 1) * LANE] = p.astype(BF16)
            v2 = vbuf[slot, i].reshape(PAGE * H_C, hd).astype(BF16)
            d = _dot(p_ref[i], v2)
            pv = d if pv is None else pv + d
        l_ref[...] = l_ref[...] * corr_prev + _row_sum(lsum)
        acc_ref[...] = acc_ref[...] * corr_prev + pv

    def both(s_w, s_r):
        m_prev = m_ref[...]
        corr_prev = corr_ref[...]
        score(s_w, m_prev)
        apply(s_r, m_prev, corr_prev)

    middle = jnp.logical_and(j > 0, j < n_steps)
    pl.when(j == 0)(lambda: score(sa_ref, m_ref[...]))
    pl.when(jnp.logical_and(middle, j % 2 == 0))(functools.partial(both, sa_ref, sb_ref))
    pl.when(jnp.logical_and(middle, j % 2 == 1))(functools.partial(both, sb_ref, sa_ref))
    last_scores = sb_ref if n_steps % 2 == 0 else sa_ref
    pl.when(j == n_steps)(lambda: apply(last_scores, m_ref[...], corr_ref[...]))

    @pl.when(j == n_steps)
    def _():
        s = lax.dot_general(qall, kn_ref[...], _NT, preferred_element_type=F32) + t1n_ref[...]
        off = slope_ref[...] * float(n_pages * PAGE)
        m_prev = m_ref[...]
        m_fin = jnp.maximum(m_prev, _row_max(s) + off)
        c = jnp.exp(m_prev - m_fin)
        p = jnp.exp(s - (m_fin - off))
        l = l_ref[...] * c + _row_sum(p)
        o = (acc_ref[...] * c + _dot(p.astype(BF16), vn_ref[...])) / l
        lam = _diff_lambda(lq1_ref, lk1_ref, lq2_ref, lk2_ref, lam_init)
        for h in range(H_C):
            od = o[2 * T * h:2 * T * h + T] - lam * o[2 * T * h + T:2 * T * (h + 1)]
            o_ref[:, hd * h:hd * (h + 1)] = (_head_rms(od) * gain_ref[...] * (1.0 - lam_init)).astype(BF16)


def _diff_sample(q, k_new, v_new, cache_k, cache_v, layer, page_table, lams, gain, lam_init):
    bsz, T, _ = q.shape
    n_pages = page_table.shape[1]
    hd = 2 * DH_C
    P = PAGES_PER_STEP
    n_rows = 2 * T * H_C
    assert n_rows == LANE and n_pages % P == 0
    c = np.arange(n_rows)
    c_h, c_t = c // (2 * T), c % T
    slope_c = 2.0 ** (-8.0 * (c_h + 1) / H_C)
    r = np.arange(PAGE * H_C)
    r_key, r_h = r // H_C, r % H_C
    same_head = r_h[None, :] == c_h[:, None]
    t1 = np.where(same_head, slope_c[:, None] * r_key[None, :], NEG)
    new_ok = same_head[:, :LANE] & (r_key[None, :LANE] <= c_t[:, None]) & (r_key[None, :LANE] < T)
    t1n = np.where(new_ok, slope_c[:, None] * r_key[None, :LANE], NEG)
    slope_tile = np.broadcast_to(slope_c[:, None], (n_rows, LANE))
    pad = lambda a: jnp.pad(a.reshape(bsz, T * H_C, hd), ((0, 0), (0, LANE - T * H_C), (0, 0)))

    n_steps = n_pages // P
    const = lambda shape: pl.BlockSpec(shape, lambda b, j, pt: (0,) * len(shape))
    per_b = lambda shape: pl.BlockSpec((None,) + shape, lambda b, j, pt: (b,) + (0,) * len(shape))
    scores = (P, n_rows, PAGE * H_C)
    stat = pltpu.VMEM((n_rows, LANE), F32)
    grid_spec = pltpu.PrefetchScalarGridSpec(
        num_scalar_prefetch=1, grid=(bsz, n_steps + 1),
        in_specs=[per_b((T, D_MODEL)), per_b((LANE, hd)), per_b((LANE, hd)),
                  const((n_rows, PAGE * H_C)), const((n_rows, LANE)), const((n_rows, LANE)),
                  const((1, DH_C)), const((1, DH_C)), const((1, DH_C)), const((1, DH_C)), const((1, hd))]
                 + [pl.BlockSpec(memory_space=pl.ANY)] * 2,
        out_specs=per_b((T, D_MODEL)),
        scratch_shapes=[pltpu.VMEM((n_rows, hd), BF16), stat, stat, stat, pltpu.VMEM((n_rows, hd), F32),
                        pltpu.VMEM(scores, F32), pltpu.VMEM(scores, F32), pltpu.VMEM(scores, BF16),
                        pltpu.VMEM((RING, P, PAGE, H_C, hd), F32), pltpu.VMEM((RING, P, PAGE, H_C, hd), F32),
                        pltpu.SemaphoreType.DMA((RING,)), pltpu.SemaphoreType.DMA((RING,))])
    return pl.pallas_call(
        functools.partial(_dsample_kernel, lam_init=lam_init, n_pages=n_pages, layer=layer),
        out_shape=jax.ShapeDtypeStruct((bsz, T, D_MODEL), BF16), grid_spec=grid_spec,
        compiler_params=_params(("arbitrary", "arbitrary"), 48), name="diff_sample")(
            page_table, q, pad(k_new), pad(v_new), jnp.asarray(t1, F32), jnp.asarray(t1n, F32),
            jnp.asarray(slope_tile, F32), *lams, gain, cache_k, cache_v)


def _cross_kernel(x_ref, g_ref, wq_ref, mk_ref, mv_ref, wo_ref, o_ref):
    G, T, _ = x_ref.shape
    x = x_ref[...].reshape(G * T, D_MODEL)
    q = _dot(_rms(x, g_ref[...]).astype(BF16), wq_ref[...])
    rows = []
    for g in range(G):
        heads = []
        scores = []
        for h in range(H_X):
            cols = slice(DH_X * h, DH_X * (h + 1))
            qh = q[g * T:(g + 1) * T, cols].astype(BF16)
            scores.append(lax.dot_general(qh, mk_ref[g, :, cols].astype(BF16), _NT,
                                          preferred_element_type=F32) * DH_X ** -0.5)
        for h in range(H_X):
            cols = slice(DH_X * h, DH_X * (h + 1))
            s = scores[h]
            p = jnp.exp(s - jnp.max(s, axis=-1, keepdims=True))
            p = p / jnp.sum(p, axis=-1, keepdims=True)
            heads.append(_dot(p.astype(BF16), mv_ref[g, :, cols].astype(BF16)))
        rows.append(jnp.concatenate(heads, axis=1))
    o = jnp.concatenate(rows, axis=0).astype(BF16)
    o_ref[...] = (x + _dot(o, wo_ref[...])).reshape(G, T, D_MODEL)


def _cross(x, gains, wq, mk, mv, wo, li, group, rows):
    bsz, L, _ = x.shape
    xs = pl.BlockSpec((group, rows, D_MODEL), lambda b, t: (b, t, 0))
    mem = pl.BlockSpec((None, group, N_MEM, D_MODEL), lambda b, t: (li, b, 0, 0))
    return pl.pallas_call(
        _cross_kernel, out_shape=jax.ShapeDtypeStruct(x.shape, F32), grid=(bsz // group, L // rows),
        in_specs=[xs, _layer(gains, li), _layer(wq, li), mem, mem, _layer(wo, li)],
        out_specs=xs, compiler_params=_params(("arbitrary", "arbitrary"), 48), name="cross")(
            x, gains, wq, mk, mv, wo)


def _cross_cache_kernel(x_ref, g_ref, wq_ref, mk_ref, mv_ref, wo_ref, o_ref):
    G, T, _ = x_ref.shape
    n_kv = N_MEM * H_X
    x = x_ref[...].reshape(G * T, D_MODEL)
    q = _dot(_rms(x, g_ref[...]).astype(BF16), wq_ref[...])
    same = (lax.broadcasted_iota(jnp.int32, (H_X * T, n_kv), 0) // T
            == lax.broadcasted_iota(jnp.int32, (H_X * T, n_kv), 1) % H_X)
    rows = []
    for g in range(G):
        qg = q[g * T:(g + 1) * T]
        q4 = jnp.concatenate([qg[:, DH_X * h:DH_X * (h + 1)] for h in range(H_X)], axis=0).astype(BF16)
        k2 = mk_ref[g].reshape(n_kv, DH_X).astype(BF16)
        v2 = mv_ref[g].reshape(n_kv, DH_X).astype(BF16)
        s = lax.dot_general(q4, k2, _NT, preferred_element_type=F32) * DH_X ** -0.5
        s = jnp.where(same, s, NEG)
        p = jnp.exp(s - jnp.max(s, axis=-1, keepdims=True))
        p = p / jnp.sum(p, axis=-1, keepdims=True)
        o4 = _dot(p.astype(BF16), v2)
        rows.append(jnp.concatenate([o4[T * h:T * (h + 1)] for h in range(H_X)], axis=1))
    o = jnp.concatenate(rows, axis=0).astype(BF16)
    o_ref[...] = (x + _dot(o, wo_ref[...])).reshape(G, T, D_MODEL)


def _cross_cache(x, gains, wq, cache_k, cache_v, wo, li, group):
    bsz, T, _ = x.shape
    xs = pl.BlockSpec((group, T, D_MODEL), lambda b: (b, 0, 0))
    mem = pl.BlockSpec((None, group, N_MEM, H_X, DH_X), lambda b: (li, b, 0, 0, 0))
    return pl.pallas_call(
        _cross_cache_kernel, out_shape=jax.ShapeDtypeStruct(x.shape, F32), grid=(bsz // group,),
        in_specs=[xs, _layer(gains, li), _layer(wq, li), mem, mem, _layer(wo, li)],
        out_specs=xs, compiler_params=_params(("arbitrary",), 56), name="cross_cache")(
            x, gains, wq, cache_k, cache_v, wo)


def kernel(x_prompt, x_sample, state_ret, state_gla, cache_k, cache_v, cache_mem_k, cache_mem_v, page_table, mem_prompt, norm_ffn1, ffn1_wg, ffn1_wu, ffn1_wd, norm_mix, w_in_even, gla_wa2, gla_ba, ret_gain, gla_gain, w_in_odd, lam_q1, lam_k1, lam_q2, lam_k2, diff_gain, w_mix_out, norm_x, x_wq, x_wk, x_wv, x_wo, norm_ffn2, ffn2_wg, ffn2_wu, ffn2_wd, final_norm):
    bp, lp, _ = x_prompt.shape
    bs, ls, _ = x_sample.shape
    depth = norm_ffn1.shape[0]
    bf = lambda w: w.astype(BF16)
    xp = x_prompt.reshape(bp * lp, D_MODEL)
    xs = x_sample.reshape(bs * ls, D_MODEL)
    slopes = jnp.asarray([2.0 ** (-8.0 * (h + 1) / H_C) for h in range(H_C)], F32)
    ret_p, ret_s, gla_p, gla_s, kr_p, vr_p, kr_s, vr_s = ([] for _ in range(8))
    rows = lambda g: g.reshape(g.shape[0], 1, g.shape[1])
    g_ffn1, g_mix, g_x, g_ffn2 = rows(norm_ffn1), rows(norm_mix), rows(norm_x), rows(norm_ffn2)
    w1 = (bf(ffn1_wg), bf(ffn1_wu), bf(ffn1_wd))
    w2 = (bf(ffn2_wg), bf(ffn2_wu), bf(ffn2_wd))
    w_out, wq, wo = bf(w_mix_out), bf(x_wq), bf(x_wo)
    w_even = bf(jnp.pad(w_in_even, ((0, 0), (0, 0), (0, Z_W - w_in_even.shape[2]))))
    w_odd = bf(w_in_odd)
    w_kv = bf(jnp.concatenate([w for li in range(depth) for w in (x_wk[li], x_wv[li])], axis=1))
    mk, mv, mk4, mv4 = _mem_proj(mem_prompt.reshape(bp * N_MEM, D_MODEL), w_kv, depth)
    mk = mk.reshape(depth, bp, N_MEM, D_MODEL)
    mv = mv.reshape(depth, bp, N_MEM, D_MODEL)

    for li in range(depth):
        xp = _ffn(xp, g_ffn1, *w1, li)
        xs = _ffn(xs, g_ffn1, *w1, li)
        if li % 2 == 0:
            e = li // 2
            wa2 = bf(jnp.pad(gla_wa2[e], ((0, LANE - GLA_RANK), (0, 0))))
            prm = (wa2, gla_ba[e].reshape(1, -1), ret_gain[e].reshape(1, -1), gla_gain[e].reshape(1, -1))
            (zp,) = _proj(xp, g_mix, li, w_even, e, ((0, Z_W, F32),))
            (zs,) = _proj(xs, g_mix, li, w_even, e, ((0, Z_W, F32),))
            zero = jnp.zeros((bp, H_A, DK_A, DV_A), F32)
            op, sr, sg = _even_mixer(zp.reshape(bp, lp, Z_W), zero, zero, *prm, valid=CHUNK)
            ret_p.append(sr); gla_p.append(sg)
            zs = jnp.pad(zs.reshape(bs, ls, Z_W), ((0, 0), (0, CHUNK - ls), (0, 0)))
            os_, sr, sg = _even_mixer(zs, state_ret[e], state_gla[e], *prm, valid=ls)
            ret_s.append(sr); gla_s.append(sg)
            os_ = os_[:, :ls]
        else:
            o = li // 2
            lam_init = 0.8 - 0.6 * math.exp(-0.3 * li)
            lams = tuple(a[o].reshape(1, DH_C) for a in (lam_q1, lam_k1, lam_q2, lam_k2))
            gain = diff_gain[o].reshape(1, 2 * DH_C)
            hd = 2 * DH_C
            outs = ((0, D_MODEL, BF16), (D_MODEL, D_MODEL, BF16), (2 * D_MODEL, D_MODEL, BF16),
                    (D_MODEL, D_MODEL, hd), (2 * D_MODEL, D_MODEL, hd))
            qp, kp, vp, kp4, vp4 = _proj(xp, g_mix, li, w_odd, o, outs)
            qs, ks, vs, ks4, vs4 = _proj(xs, g_mix, li, w_odd, o, outs)
            shp = lambda a: a.reshape(bp, lp, D_MODEL)
            shs = lambda a: a.reshape(bs, ls, D_MODEL)
            op = _diff_prompt(shp(qp), shp(kp), shp(vp), slopes, lams, gain, lam_init)
            os_ = _diff_sample(shs(qs), shs(ks), shs(vs), cache_k, cache_v, o, page_table, lams, gain, lam_init)
            kr_p.append(kp4.reshape(bp, lp, H_C, hd)); vr_p.append(vp4.reshape(bp, lp, H_C, hd))
            kr_s.append(ks4.reshape(bs, ls, H_C, hd)); vr_s.append(vs4.reshape(bs, ls, H_C, hd))
        xp = _outproj(op.reshape(bp * lp, D_MODEL), w_out, li, xp)
        xs = _outproj(os_.reshape(bs * ls, D_MODEL), w_out, li, xs)
        xp = _cross(xp.reshape(bp, lp, D_MODEL), g_x, wq, mk, mv, wo, li, group=1, rows=512).reshape(bp * lp, D_MODEL)
        xs = _cross_cache(xs.reshape(bs, ls, D_MODEL), g_x, wq, cache_mem_k, cache_mem_v, wo, li,
                          group=4).reshape(bs * ls, D_MODEL)
        fg = final_norm if li == depth - 1 else None
        xp = _ffn(xp, g_ffn2, *w2, li, final_gain=fg)
        xs = _ffn(xs, g_ffn2, *w2, li, final_gain=fg)

    return (xp.reshape(bp, lp, D_MODEL), xs.reshape(bs, ls, D_MODEL),
            jnp.stack(ret_p), jnp.stack(ret_s), jnp.stack(gla_p), jnp.stack(gla_s),
            jnp.stack(kr_p), jnp.stack(vr_p), jnp.stack(kr_s), jnp.stack(vr_s),
            mk4.reshape(depth, bp, N_MEM, H_X, DH_X), mv4.reshape(depth, bp, N_MEM, H_X, DH_X))
```
